```python
import math
import jax, jax.numpy as jnp
from jax import lax
import numpy as np

D_MODEL = 1024
BATCH = 8
SEQ = 4096
DEPTH = 2

N_BRANCH = 4
BRANCH_W = D_MODEL // 4
HEAD_DIM = 64
POOL_WINDOWS = (2, 4, 8, 16)
POOL_GROUP = BRANCH_W // len(POOL_WINDOWS)
FOX_HEADS = BRANCH_W // HEAD_DIM
DSA_HEADS = BRANCH_W // HEAD_DIM
IDX_HEADS = 4
IDX_DIM = 64
DSA_TOPK = 256
CONV_W = 31
CONV_CH = BRANCH_W
N_GROUPS = 4
EXPERTS_PER_GROUP = 4
N_EXPERTS = N_GROUPS * EXPERTS_PER_GROUP
TOP_K_INNER = 2
D_FF_EXPERT = D_MODEL // 2

Q_BLOCK = 128
LN_EPS = 1e-5
NEG_INF = -1e30
DEEPNORM_ALPHA = (2.0 * DEPTH) ** 0.25
DEEPNORM_BETA = (8.0 * DEPTH) ** -0.25

SPLIT_SIZES = (
    BRANCH_W,
    FOX_HEADS * HEAD_DIM,
    FOX_HEADS * HEAD_DIM,
    FOX_HEADS * HEAD_DIM,
    FOX_HEADS,
    DSA_HEADS * HEAD_DIM,
    HEAD_DIM,
    HEAD_DIM,
    IDX_HEADS * IDX_DIM,
    IDX_DIM,
    IDX_HEADS,
    2 * CONV_CH,
    N_BRANCH * D_MODEL,
)
P_TOTAL = sum(SPLIT_SIZES)

kernel_name = "hybrid_pool_fox_dsa_conv_hmoe_deepnorm"


def layer_norm(x, g, b):
    xf = x.astype(jnp.float32)
    mu = jnp.mean(xf, axis=-1, keepdims=True)
    var = jnp.mean(jnp.square(xf - mu), axis=-1, keepdims=True)
    return ((xf - mu) * lax.rsqrt(var + LN_EPS) * g + b).astype(x.dtype)


def split_columns(p):
    outs, off = [], 0
    for n in SPLIT_SIZES:
        outs.append(p[..., off:off + n])
        off += n
    return outs


def pool_mixer(v, pool_w, pool_scale):
    B, S, C = v.shape
    vf = v.astype(jnp.float32)
    cs = jnp.cumsum(vf, axis=1)
    pos = jnp.arange(1, S + 1, dtype=jnp.float32)
    outs = []
    for g, w in enumerate(POOL_WINDOWS):
        sl = slice(g * POOL_GROUP, (g + 1) * POOL_GROUP)
        csg = cs[..., sl]
        lag = jnp.pad(csg, ((0, 0), (w, 0), (0, 0)))[:, :S]
        cnt = jnp.minimum(pos, float(w))[None, :, None]
        outs.append((csg - lag) / cnt - vf[..., sl])
    pooled = jnp.stack(outs, axis=2).astype(v.dtype)
    mixed = jnp.einsum('bsgc,gcd->bsgd', pooled, pool_w).reshape(B, S, C)
    return mixed * pool_scale


def fox_attention(q, k, v, f_logit):
    B, S, H, Dh = q.shape
    nb = S // Q_BLOCK
    scale = HEAD_DIM ** -0.5
    c = jnp.cumsum(jax.nn.log_sigmoid(f_logit.astype(jnp.float32)), axis=1)
    c_k = c.transpose(0, 2, 1)
    qb = q.reshape(B, nb, Q_BLOCK, H, Dh).swapaxes(0, 1)
    cb = c.reshape(B, nb, Q_BLOCK, H).swapaxes(0, 1)
    kpos = jnp.arange(S)

    def block(args):
        qi, ci, start = args
        qpos = start + jnp.arange(Q_BLOCK)
        s = jnp.einsum('bqhd,bkhd->bhqk', qi, k).astype(jnp.float32) * scale
        s = s + ci.transpose(0, 2, 1)[..., None] - c_k[:, :, None, :]
        s = jnp.where(kpos[None, :] <= qpos[:, None], s, NEG_INF)
        p = jax.nn.softmax(s, axis=-1).astype(v.dtype)
        return jnp.einsum('bhqk,bkhd->bqhd', p, v)

    out = lax.map(block, (qb, cb, jnp.arange(nb) * Q_BLOCK))
    return out.swapaxes(0, 1).reshape(B, S, H * Dh)


def dsa_attention(q, k, v, q_idx, k_idx, w_idx):
    B, S, H, Dh = q.shape
    nb = S // Q_BLOCK
    topk = min(DSA_TOPK, S // 4)
    scale = HEAD_DIM ** -0.5
    slopes = jnp.asarray([2.0 ** (-8.0 * (h + 1) / DSA_HEADS) for h in range(DSA_HEADS)],
                         dtype=jnp.float32)
    qb = q.reshape(B, nb, Q_BLOCK, H, Dh).swapaxes(0, 1)
    qib = q_idx.reshape(B, nb, Q_BLOCK, IDX_HEADS, IDX_DIM).swapaxes(0, 1)
    wib = w_idx.reshape(B, nb, Q_BLOCK, IDX_HEADS).swapaxes(0, 1)
    kpos = jnp.arange(S)
    gather = jax.vmap(lambda a, i: a[i])

    def block(args):
        qi, qii, wi, start = args
        qpos = start + jnp.arange(Q_BLOCK)
        rel = jax.nn.relu(jnp.einsum('bqhd,bkd->bqhk', qii, k_idx).astype(jnp.float32))
        score = jnp.einsum('bqhk,bqh->bqk', rel, wi.astype(jnp.float32))
        score = jnp.where(kpos[None, None, :] <= qpos[None, :, None], score, NEG_INF)
        _, idx = lax.top_k(score, topk)
        k_sel = gather(k, idx)
        v_sel = gather(v, idx)
        logits = jnp.einsum('bqhd,bqkd->bhqk', qi, k_sel).astype(jnp.float32) * scale
        dist = (qpos[None, :, None] - idx).astype(jnp.float32)
        logits = logits - slopes[None, :, None, None] * dist[:, None]
        valid = (idx <= qpos[None, :, None])[:, None]
        logits = jnp.where(valid, logits, NEG_INF)
        p = jax.nn.softmax(logits, axis=-1).astype(v.dtype)
        return jnp.einsum('bhqk,bqkd->bqhd', p, v_sel)

    out = lax.map(block, (qb, qib, wib, jnp.arange(nb) * Q_BLOCK))
    return out.swapaxes(0, 1).reshape(B, S, H * Dh)


def conv_module(a, conv_dw, conv_b, ln_g, ln_b):
    C = CONV_CH
    h = a[..., :C] * jax.nn.sigmoid(a[..., C:])
    y = lax.conv_general_dilated(h, conv_dw[:, None, :].astype(h.dtype), window_strides=(1,),
                                 padding=((CONV_W - 1, 0),),
                                 dimension_numbers=('NWC', 'WIO', 'NWC'),
                                 feature_group_count=C) + conv_b
    return jax.nn.silu(layer_norm(y, ln_g, ln_b))


def hier_moe(x, rg, rg_b, re, re_b, wg, wu, wd):
    B, S, D = x.shape
    t = x.reshape(-1, D)
    pg = jax.nn.softmax((t @ rg).astype(jnp.float32) + rg_b, axis=-1)
    pg_top, g_idx = lax.top_k(pg, 1)
    le = ((t @ re).astype(jnp.float32) + re_b).reshape(-1, N_GROUPS, EXPERTS_PER_GROUP)
    le = jnp.take_along_axis(le, g_idx[:, :, None], axis=1)[:, 0]
    pe_top, e_idx = lax.top_k(jax.nn.softmax(le, axis=-1), TOP_K_INNER)
    pe_top = pe_top / jnp.sum(pe_top, axis=-1, keepdims=True)
    gate = pg_top * pe_top
    expert_id = g_idx * EXPERTS_PER_GROUP + e_idx
    combine = jnp.sum(jax.nn.one_hot(expert_id, N_EXPERTS, dtype=jnp.float32) * gate[..., None],
                      axis=1)
    out = jnp.zeros(t.shape, jnp.float32)
    for e in range(N_EXPERTS):
        h = jax.nn.silu(t @ wg[e]) * (t @ wu[e])
        out = out + combine[:, e:e + 1] * (h @ wd[e]).astype(jnp.float32)
    return out.astype(x.dtype).reshape(B, S, D)


def setup_inputs(seed: int = 0) -> dict:
    key = jax.random.key(seed)
    ks = jax.random.split(key, 32)
    L, D = DEPTH, D_MODEL
    f32 = jnp.float32
    nrm = lambda k, shape, s: (jax.random.normal(k, shape, f32) * s)
    return {
        "x": nrm(ks[0], (BATCH, SEQ, D), 1.0),
        "ln_in_g": 1.0 + nrm(ks[1], (D,), 0.05),
        "ln_in_b": nrm(ks[2], (D,), 0.05),
        "w_in": nrm(ks[3], (L, D, P_TOTAL), D ** -0.5),
        "b_forget": 3.0 + nrm(ks[4], (L, FOX_HEADS), 0.5),
        "b_gate": nrm(ks[5], (L, N_BRANCH * D), 0.1),
        "pool_w": nrm(ks[6], (L, len(POOL_WINDOWS), POOL_GROUP, POOL_GROUP), POOL_GROUP ** -0.5),
        "pool_scale": 1.0 + nrm(ks[7], (L, BRANCH_W), 0.1),
        "conv_dw": nrm(ks[8], (L, CONV_W, CONV_CH), CONV_W ** -0.5),
        "conv_b": nrm(ks[9], (L, CONV_CH), 0.02),
        "conv_ln_g": 1.0 + nrm(ks[10], (L, CONV_CH), 0.05),
        "conv_ln_b": nrm(ks[11], (L, CONV_CH), 0.05),
        "w_branch": nrm(ks[12], (L, N_BRANCH, BRANCH_W, D), BRANCH_W ** -0.5),
        "w_out": nrm(ks[13], (L, D, D), D ** -0.5 * DEEPNORM_BETA),
        "ln1_g": 1.0 + nrm(ks[14], (L, D), 0.05),
        "ln1_b": nrm(ks[15], (L, D), 0.05),
        "router_g": nrm(ks[16], (L, D, N_GROUPS), D ** -0.5),
        "router_g_b": nrm(ks[17], (L, N_GROUPS), 0.01),
        "router_e": nrm(ks[18], (L, D, N_EXPERTS), D ** -0.5),
        "router_e_b": nrm(ks[19], (L, N_EXPERTS), 0.01),
        "expert_w_gate": nrm(ks[20], (L, N_EXPERTS, D, D_FF_EXPERT), D ** -0.5),
        "expert_w_up": nrm(ks[21], (L, N_EXPERTS, D, D_FF_EXPERT), D ** -0.5),
        "expert_w_down": nrm(ks[22], (L, N_EXPERTS, D_FF_EXPERT, D), D_FF_EXPERT ** -0.5 * DEEPNORM_BETA),
        "ln2_g": 1.0 + nrm(ks[23], (L, D), 0.05),
        "ln2_b": nrm(ks[24], (L, D), 0.05),
    }


def reference(x, ln_in_g, ln_in_b, w_in, b_forget, b_gate, pool_w, pool_scale, conv_dw, conv_b,
              conv_ln_g, conv_ln_b, w_branch, w_out, ln1_g, ln1_b, router_g, router_g_b,
              router_e, router_e_b, expert_w_gate, expert_w_up, expert_w_down, ln2_g, ln2_b):
    B, S, D = x.shape
    h = layer_norm(x, ln_in_g, ln_in_b)
    for l in range(DEPTH):
        (pool_v, fq, fk, fv, ff, dq, dk, dv, iq, ik, iw, conv_in, gate) = split_columns(h @ w_in[l])
        br_a = pool_mixer(pool_v, pool_w[l], pool_scale[l])
        br_b = fox_attention(fq.reshape(B, S, FOX_HEADS, HEAD_DIM),
                             fk.reshape(B, S, FOX_HEADS, HEAD_DIM),
                             fv.reshape(B, S, FOX_HEADS, HEAD_DIM),
                             ff + b_forget[l])
        br_c = dsa_attention(dq.reshape(B, S, DSA_HEADS, HEAD_DIM), dk, dv,
                             iq.reshape(B, S, IDX_HEADS, IDX_DIM), ik, iw)
        br_d = conv_module(conv_in, conv_dw[l], conv_b[l], conv_ln_g[l], conv_ln_b[l])
        branches = jnp.stack([br_a, br_b.astype(br_a.dtype), br_c.astype(br_a.dtype),
                              br_d.astype(br_a.dtype)], axis=2)
        up = jnp.einsum('bsnc,ncd->bsnd', branches, w_branch[l])
        g = jax.nn.sigmoid(gate.reshape(B, S, N_BRANCH, D) + b_gate[l].reshape(N_BRANCH, D))
        mixed = jnp.sum(g * up, axis=2) @ w_out[l]
        h = layer_norm(DEEPNORM_ALPHA * h + mixed, ln1_g[l], ln1_b[l])
        moe = hier_moe(h, router_g[l], router_g_b[l], router_e[l], router_e_b[l],
                       expert_w_gate[l], expert_w_up[l], expert_w_down[l])
        h = layer_norm(DEEPNORM_ALPHA * h + moe, ln2_g[l], ln2_b[l])
    return h
```

```python
import functools

import jax
import jax.numpy as jnp
from jax import lax
from jax.experimental import pallas as pl
from jax.experimental.pallas import tpu as pltpu

F32 = jnp.float32
BF16 = jnp.bfloat16
I32 = jnp.int32

N_BRANCH = 4
HEAD_DIM = 64
N_HEADS = 4
BRANCH_W = N_HEADS * HEAD_DIM
POOL_WINDOWS = (2, 4, 8, 16)
POOL_GROUP = BRANCH_W // len(POOL_WINDOWS)
CONV_W = 31
DSA_TOPK = 256
N_GROUPS = 4
EXPERTS_PER_GROUP = 4
N_EXPERTS = N_GROUPS * EXPERTS_PER_GROUP
LN_EPS = 1e-5
NEG_INF = -1e30
ATTN_SCALE = HEAD_DIM ** -0.5

LANES = 128
HALO = 32
VMEM_LIMIT = 56 * 1024 * 1024

_OFF = {}
_o = 0
for _name, _n in (("pool", BRANCH_W), ("fq", BRANCH_W), ("fk", BRANCH_W), ("fv", BRANCH_W),
                  ("ff", N_HEADS), ("dq", BRANCH_W), ("dk", HEAD_DIM), ("dv", HEAD_DIM),
                  ("iq", BRANCH_W), ("ik", HEAD_DIM), ("iw", N_HEADS), ("conv", 2 * BRANCH_W)):
    _OFF[_name] = (_o, _o + _n)
    _o += _n
GATE_OFF = _o
SMALL_FF = HEAD_DIM
SMALL_IW = HEAD_DIM + N_HEADS


def _cparams(*sem):
    return pltpu.CompilerParams(dimension_semantics=sem, vmem_limit_bytes=VMEM_LIMIT)


def _ln_rows(x, g, b):
    mu = jnp.mean(x, axis=-1, keepdims=True)
    xc = x - mu
    var = jnp.mean(xc * xc, axis=-1, keepdims=True)
    return xc * lax.rsqrt(var + LN_EPS) * g + b


def _sigmoid(x):
    return 1.0 / (1.0 + jnp.exp(-x))


def _dot(a, b):
    return jnp.dot(a, b, preferred_element_type=F32)


def _dot_nt(a, b):
    return lax.dot_general(a, b, (((1,), (1,)), ((), ())), preferred_element_type=F32)


def _ln_kernel(x_ref, g_ref, b_ref, o_ref):
    o_ref[...] = _ln_rows(x_ref[...], g_ref[...], b_ref[...])


def _layer_norm(x2d, g, b, tm):
    n, d = x2d.shape
    return pl.pallas_call(
        _ln_kernel,
        grid=(n // tm,),
        in_specs=[pl.BlockSpec((tm, d), lambda i: (i, 0)),
                  pl.BlockSpec((1, d), lambda i: (0, 0)),
                  pl.BlockSpec((1, d), lambda i: (0, 0))],
        out_specs=pl.BlockSpec((tm, d), lambda i: (i, 0)),
        out_shape=jax.ShapeDtypeStruct((n, d), F32),
        compiler_params=_cparams("parallel"),
        name="ln_in",
    )(x2d, g.reshape(1, d), b.reshape(1, d))


_PK = {"pool": 0, "fq": 256, "fk": 512, "fv": 768, "dq": 1024, "iq": 1280, "conv": 1536,
       "dkv": 2048, "small": 2176}
PK_COLS = 2304


def _pack_proj_weight(w):
    d = w.shape[0]
    sl = lambda k: w[:, _OFF[k][0]:_OFF[k][1]]
    small = jnp.concatenate([sl("ik"), sl("ff"), sl("iw"),
                             jnp.zeros((d, LANES - HEAD_DIM - 2 * N_HEADS), w.dtype)], axis=1)
    packed = jnp.concatenate([sl("pool"), sl("fq") * ATTN_SCALE, sl("fk"), sl("fv"),
                              sl("dq") * ATTN_SCALE, sl("iq"), sl("conv"), sl("dk"), sl("dv"),
                              small], axis=1)
    return packed.astype(BF16)


def _proj_kernel(h_ref, w_ref, pool_o, fq_o, fk_o, fv_o, dq_o, iq_o, glu_o, dk_o, dv_o, ik_o,
                 small_o):
    hb = h_ref[0].astype(BF16)

    def piece(name, width):
        return _dot(hb, w_ref[:, _PK[name]:_PK[name] + width])

    pool_o[0] = piece("pool", BRANCH_W)
    for name, out in (("fq", fq_o), ("fk", fk_o), ("fv", fv_o), ("dq", dq_o), ("iq", iq_o)):
        r = piece(name, BRANCH_W).astype(BF16)
        for h in range(N_HEADS):
            out[0, h] = r[:, h * HEAD_DIM:(h + 1) * HEAD_DIM]
    a = piece("conv", 2 * BRANCH_W)
    glu_o[0] = a[:, :BRANCH_W] * _sigmoid(a[:, BRANCH_W:])
    kv = piece("dkv", 2 * HEAD_DIM).astype(BF16)
    dk_o[0] = kv[:, :HEAD_DIM]
    dv_o[0] = kv[:, HEAD_DIM:]
    sm = piece("small", LANES)
    small_o[0] = sm
    ik_o[0] = sm[:, :HEAD_DIM].astype(BF16)


def _projection(h, wp, tm):
    b, s, d = h.shape
    heads = lambda: pl.BlockSpec((1, N_HEADS, tm, HEAD_DIM), lambda bi, i: (bi, 0, i, 0))
    rows = lambda w: pl.BlockSpec((1, tm, w), lambda bi, i: (bi, i, 0))
    head_shape = jax.ShapeDtypeStruct((b, N_HEADS, s, HEAD_DIM), BF16)
    return pl.pallas_call(
        _proj_kernel,
        grid=(b, s // tm),
        in_specs=[rows(d), pl.BlockSpec((d, PK_COLS), lambda bi, i: (0, 0))],
        out_specs=[rows(BRANCH_W), heads(), heads(), heads(), heads(), heads(), rows(BRANCH_W),
                   rows(HEAD_DIM), rows(HEAD_DIM), rows(HEAD_DIM), rows(LANES)],
        out_shape=[jax.ShapeDtypeStruct((b, s, BRANCH_W), F32), head_shape, head_shape, head_shape,
                   head_shape, head_shape, jax.ShapeDtypeStruct((b, s, BRANCH_W), F32),
                   jax.ShapeDtypeStruct((b, s, HEAD_DIM), BF16),
                   jax.ShapeDtypeStruct((b, s, HEAD_DIM), BF16),
                   jax.ShapeDtypeStruct((b, s, HEAD_DIM), BF16),
                   jax.ShapeDtypeStruct((b, s, LANES), F32)],
        compiler_params=_cparams("parallel", "parallel"),
        name="proj",
    )(h, wp)


def _forget_cumsum_kernel(f_ref, b_ref, o_ref):
    s = f_ref.shape[2]
    x = f_ref[0] + b_ref[...]
    ls = jnp.minimum(x, 0.0) - jnp.log(1.0 + jnp.exp(-jnp.abs(x)))
    r = lax.broadcasted_iota(I32, (LANES, LANES), 0)
    c = lax.broadcasted_iota(I32, (LANES, LANES), 1)
    upper = jnp.where(r <= c, 1.0, 0.0).astype(F32)
    carry = jnp.zeros((x.shape[0], 1), F32)
    for j in range(s // LANES):
        blk = jnp.dot(ls[:, j * LANES:(j + 1) * LANES], upper, preferred_element_type=F32,
                      precision=lax.Precision.HIGHEST) + carry
        carry = blk[:, LANES - 1:LANES]
        rest = blk
        for piece in range(3):
            part = rest.astype(BF16)
            o_ref[0, piece, :, j * LANES:(j + 1) * LANES] = part
            rest = rest - part.astype(F32)


def _forget_cumsum(f_t, b_col):
    b, r, s = f_t.shape
    return pl.pallas_call(
        _forget_cumsum_kernel,
        grid=(b,),
        in_specs=[pl.BlockSpec((1, r, s), lambda bi: (bi, 0, 0)),
                  pl.BlockSpec((r, 1), lambda bi: (0, 0))],
        out_specs=pl.BlockSpec((1, 3, r, s), lambda bi: (bi, 0, 0, 0)),
        out_shape=jax.ShapeDtypeStruct((b, 3, r, s), BF16),
        compiler_params=_cparams("parallel"),
        name="forget_cumsum",
    )(f_t, b_col)


def _local_kernel(pv_ref, pvh_ref, gl_ref, glh_ref, pw_ref, ps_ref, dw_ref, cb_ref, g_ref, b_ref,
                  a_o, d_o, pbuf, cbuf, *, ts):
    i = pl.program_id(1)
    first = i == 0
    pbuf[0:HALO, :] = jnp.where(first, 0.0, pvh_ref[0])
    pbuf[HALO:HALO + ts, :] = pv_ref[0]
    cbuf[0:HALO, :] = jnp.where(first, 0.0, glh_ref[0])
    cbuf[HALO:HALO + ts, :] = gl_ref[0]

    lane = lax.broadcasted_iota(I32, (ts, BRANCH_W), 1)
    pos = (i * ts + 1 + lax.broadcasted_iota(I32, (ts, 1), 0)).astype(F32)
    v = pbuf[HALO:HALO + ts, :]
    run = v
    pooled = jnp.zeros((ts, BRANCH_W), F32)
    shift = 1
    for g, w in enumerate(POOL_WINDOWS):
        while shift < w:
            run = run + pbuf[HALO - shift:HALO - shift + ts, :]
            shift += 1
        in_group = (lane >= g * POOL_GROUP) & (lane < (g + 1) * POOL_GROUP)
        pooled = jnp.where(in_group, run / jnp.minimum(pos, float(w)), pooled)
    pooled = pooled - v
    a_o[0] = (_dot(pooled.astype(BF16), pw_ref[...]) * ps_ref[...]).astype(BF16)

    y = jnp.zeros((ts, BRANCH_W), F32) + cb_ref[...]
    for j in range(CONV_W):
        back = CONV_W - 1 - j
        y = y + cbuf[HALO - back:HALO - back + ts, :] * dw_ref[j:j + 1, :]
    z = _ln_rows(y, g_ref[...], b_ref[...])
    d_o[0] = (z * _sigmoid(z)).astype(BF16)


def _local_mixers(pool_v, glu, pool_wbd, pool_scale, conv_dw, conv_b, ln_g, ln_b, ts):
    b, s, c = pool_v.shape
    cur = pl.BlockSpec((1, ts, c), lambda bi, i: (bi, i, 0))
    halo = pl.BlockSpec((1, HALO, c), lambda bi, i: (bi, jnp.maximum(i * (ts // HALO) - 1, 0), 0))
    full2 = lambda shp: pl.BlockSpec(shp, lambda bi, i: (0, 0))
    out_shape = jax.ShapeDtypeStruct((b, s, c), BF16)
    return pl.pallas_call(
        functools.partial(_local_kernel, ts=ts),
        grid=(b, s // ts),
        in_specs=[cur, halo, cur, halo, full2((c, c)), full2((1, c)), full2((CONV_W, c)),
                  full2((1, c)), full2((1, c)), full2((1, c))],
        out_specs=[cur, cur],
        out_shape=[out_shape, out_shape],
        scratch_shapes=[pltpu.VMEM((HALO + ts, c), F32), pltpu.VMEM((HALO + ts, c), F32)],
        compiler_params=_cparams("parallel", "parallel"),
        name="local_mixers",
    )(pool_v, pool_v, glu, glu, pool_wbd, pool_scale.reshape(1, c), conv_dw, conv_b.reshape(1, c),
      ln_g.reshape(1, c), ln_b.reshape(1, c))


AUG = 2 * HEAD_DIM


SUBLANES = 8
FOLD_CHAINS = 4


def _fold_rows(x, op):
    slabs = [x[i:i + SUBLANES] for i in range(0, x.shape[0], SUBLANES)]
    accs = slabs[:FOLD_CHAINS]
    for i, slab in enumerate(slabs[FOLD_CHAINS:]):
        accs[i % FOLD_CHAINS] = op(accs[i % FOLD_CHAINS], slab)
    while len(accs) > 1:
        accs = [op(accs[i], accs[i + 1]) for i in range(0, len(accs) - 1, 2)] + (
            [accs[-1]] if len(accs) % 2 else [])
    return accs[0]


def _softmax_chunk(s, m, l):
    m_new = jnp.maximum(m, jnp.max(_fold_rows(s, jnp.maximum), axis=0, keepdims=True))
    alpha = jnp.exp(m - m_new)
    p = jnp.exp(s - m_new)
    l_new = alpha * l + jnp.sum(_fold_rows(p, jnp.add), axis=0, keepdims=True)
    return m_new, l_new, alpha, p.astype(BF16)


def _causal_split(qi, tq, ck):
    return (qi * tq + 1) // ck, ((qi + 1) * tq + ck - 1) // ck


CHUNKS_PER_STEP = 2


def _attention_loops(n_plain, n_chunks, tq, scores, edit_last, values):
    def step(j0, carry, n, edit):
        ss = [scores(j0 + c) for c in range(n)]
        carry = list(carry)
        for c in range(n):
            stats, ps = [], []
            for h in range(N_HEADS):
                s = edit_last(j0 + c, h, ss[c][h]) if edit else ss[c][h]
                m_new, l_new, alpha, p = _softmax_chunk(s, carry[h][0], carry[h][1])
                stats.append((m_new, l_new, alpha))
                ps.append(p)
            for h in range(N_HEADS):
                m_new, l_new, alpha = stats[h]
                carry[h] = (m_new, l_new, alpha * carry[h][2] + _dot(values(j0 + c, h), ps[h]))
        return tuple(carry)

    carry = tuple((jnp.full((1, tq), NEG_INF, F32), jnp.zeros((1, tq), F32),
                   jnp.zeros((HEAD_DIM, tq), F32)) for _ in range(N_HEADS))
    n_multi = n_plain // CHUNKS_PER_STEP
    carry = lax.fori_loop(
        0, n_multi, lambda i, c: step(i * CHUNKS_PER_STEP, c, CHUNKS_PER_STEP, False), carry)
    carry = lax.fori_loop(n_multi * CHUNKS_PER_STEP, n_plain,
                          lambda j, c: step(j, c, 1, False), carry)
    if edit_last is None:
        return carry
    return lax.fori_loop(n_plain, n_chunks, lambda j, c: step(j, c, 1, True), carry)


def _fox_kernel(q_ref, k_ref, vt_ref, o_ref, *, tq, ck):
    qi = pl.program_id(1)
    n_full, n_chunks = _causal_split(qi, tq, ck)
    qpos = qi * tq + lax.broadcasted_iota(I32, (1, tq), 1)
    kiota = lax.broadcasted_iota(I32, (ck, 1), 0)
    qs = [q_ref[0, h] for h in range(N_HEADS)]

    def scores(j):
        start = pl.multiple_of(j * ck, ck)
        return [_dot_nt(k_ref[0, h, pl.ds(start, ck), :], qs[h]) for h in range(N_HEADS)]

    def causal(j, h, s):
        return jnp.where(j * ck + kiota <= qpos, s, NEG_INF)

    def values(j, h):
        return vt_ref[0, h, j]

    carry = _attention_loops(n_full, n_chunks, tq, scores, causal, values)
    for h in range(N_HEADS):
        _, l, acc = carry[h]
        o_ref[0, h * HEAD_DIM:(h + 1) * HEAD_DIM, :] = (acc / l).astype(BF16)


def _fox_attention(fq, fk, fv, c_parts, tq, ck):
    b, nh, s, dh = fq.shape
    n_ck = s // ck
    pad = jnp.zeros((b, nh, s, AUG - dh - 3), BF16)
    q_aug = jnp.concatenate([fq, jnp.full((b, nh, s, 3), -1.0, BF16), pad], axis=-1)
    k_aug = jnp.concatenate([fk, jnp.moveaxis(c_parts, 1, 3), pad], axis=-1)
    v_t = jnp.swapaxes(fv.reshape(b, nh, n_ck, ck, dh), 3, 4)
    return pl.pallas_call(
        functools.partial(_fox_kernel, tq=tq, ck=ck),
        grid=(b, s // tq),
        in_specs=[pl.BlockSpec((1, nh, tq, AUG), lambda bi, i: (bi, 0, i, 0)),
                  pl.BlockSpec((1, nh, s, AUG), lambda bi, i: (bi, 0, 0, 0)),
                  pl.BlockSpec((1, nh, n_ck, dh, ck), lambda bi, i: (bi, 0, 0, 0, 0))],
        out_specs=pl.BlockSpec((1, nh * dh, tq), lambda bi, i: (bi, 0, i)),
        out_shape=jax.ShapeDtypeStruct((b, nh * dh, s), BF16),
        compiler_params=_cparams("parallel", "arbitrary"),
        name="fox_attention",
    )(q_aug, k_aug, v_t)


def _key_to_f32(key):
    bits = key ^ (lax.shift_right_arithmetic(key, 31) & 0x7FFFFFFF)
    return lax.bitcast_convert_type(bits, F32)


def _dsa_kernel(dq_ref, iq_ref, iwt_ref, ik_ref, dk_ref, dvt_ref, o_ref, score_buf, bias_buf, *,
                tq, ck, topk):
    qi = pl.program_id(1)
    n_full, n_chunks = _causal_split(qi, tq, ck)
    qpos = qi * tq + lax.broadcasted_iota(I32, (1, tq), 1)
    kiota = lax.broadcasted_iota(I32, (ck, 1), 0)
    w_idx = iwt_ref[0]
    kf = float(topk)

    def count(pred):
        def cb(j, acc):
            return acc + _fold_rows(jnp.where(pred(score_buf[j]), 1.0, 0.0), jnp.add)
        acc = lax.fori_loop(0, n_chunks, cb, jnp.zeros((SUBLANES, tq), F32))
        return jnp.sum(acc, axis=0, keepdims=True)

    def score_body(j, _, masked):
        start = pl.multiple_of(j * ck, ck)
        ikc = ik_ref[0, pl.ds(start, ck), :]
        sc = None
        for h in range(N_HEADS):
            term = jnp.maximum(_dot_nt(ikc, iq_ref[0, h]), 0.0) * w_idx[h:h + 1, :]
            sc = term if sc is None else sc + term
        if masked:
            sc = jnp.where(start + kiota <= qpos, sc, NEG_INF)
        score_buf[j] = sc
        return 0

    lax.fori_loop(0, n_full, functools.partial(score_body, masked=False), 0)
    lax.fori_loop(n_full, n_chunks, functools.partial(score_body, masked=True), 0)

    int_min = jnp.full((1, tq), -2 ** 31, I32)
    c0 = count(lambda s: s >= 0.0)
    prefix = jnp.where(c0 >= kf, 0, int_min)

    def bit_body(i, prefix):
        cand = prefix | lax.shift_left(jnp.int32(1), 30 - i)
        cand_f = _key_to_f32(cand)
        c = count(lambda s: s >= cand_f)
        return jnp.where(c >= kf, cand, prefix)

    thr = _key_to_f32(lax.fori_loop(0, 31, bit_body, prefix))

    need = kf - count(lambda s: s > thr)
    r_i = lax.broadcasted_iota(I32, (ck, ck), 0)
    c_i = lax.broadcasted_iota(I32, (ck, ck), 1)
    lower = jnp.where(c_i <= r_i, 1.0, 0.0).astype(BF16)

    def bias_body(j, seen, masked):
        start = pl.multiple_of(j * ck, ck)
        sc = score_buf[j]
        tie = sc == thr
        rank = _dot(lower, jnp.where(tie, 1.0, 0.0).astype(BF16)) + seen
        keep = jnp.where(sc > thr, 0.0, jnp.where(tie, jnp.where(rank <= need, 0.0, NEG_INF),
                                                  NEG_INF))
        if masked:
            keep = jnp.where(start + kiota <= qpos, keep, NEG_INF)
        bias_buf[j] = keep
        return rank[ck - 1:ck, :]

    seen = lax.fori_loop(0, n_full, functools.partial(bias_body, masked=False),
                         jnp.zeros((1, tq), F32))
    lax.fori_loop(n_full, n_chunks, functools.partial(bias_body, masked=True), seen)

    qs = [dq_ref[0, h] for h in range(N_HEADS)]

    def scores(j):
        start = pl.multiple_of(j * ck, ck)
        kc = dk_ref[0, pl.ds(start, ck), :]
        bias = bias_buf[j]
        return [_dot_nt(kc, qs[h]) + bias for h in range(N_HEADS)]

    carry = _attention_loops(n_chunks, n_chunks, tq, scores, None, lambda j, h: dvt_ref[0, j])
    for h in range(N_HEADS):
        _, l, acc = carry[h]
        o_ref[0, h * HEAD_DIM:(h + 1) * HEAD_DIM, :] = (acc / l).astype(BF16)


def _alibi_columns(s):
    pos = jnp.arange(s, dtype=I32)
    hi, lo = (pos // 64).astype(F32), (pos % 64).astype(F32)
    one = jnp.ones((s,), F32)
    k_cols = jnp.stack([hi, lo, one, one], axis=-1)
    q_cols = []
    for h in range(N_HEADS):
        slope = 2.0 ** (-8.0 * (h + 1) / N_HEADS)
        q_cols.append(jnp.stack([64.0 * slope * one, slope * one, -64.0 * slope * hi, -slope * lo],
                                axis=-1))
    return k_cols.astype(BF16), jnp.stack(q_cols, axis=0).astype(BF16)


def _dsa_attention(dq, iq, iw_t, ik, dk, dv, tq, ck):
    b, nh, s, dh = dq.shape
    assert s <= 64 * 64, "position columns hold two base-64 digits"
    topk = min(DSA_TOPK, s // 4)
    assert ck >= topk, "every query must see at least topk (possibly masked) candidates"
    n_ck = s // ck
    k_cols, q_cols = _alibi_columns(s)
    q_aug = jnp.concatenate([dq, jnp.broadcast_to(q_cols[None], (b, nh, s, 4)),
                             jnp.zeros((b, nh, s, AUG - dh - 4), BF16)], axis=-1)
    k_aug = jnp.concatenate([dk, jnp.broadcast_to(k_cols[None], (b, s, 4)),
                             jnp.zeros((b, s, AUG - dh - 4), BF16)], axis=-1)
    v_t = jnp.swapaxes(dv.reshape(b, n_ck, ck, dh), 2, 3)
    return pl.pallas_call(
        functools.partial(_dsa_kernel, tq=tq, ck=ck, topk=topk),
        grid=(b, s // tq),
        in_specs=[pl.BlockSpec((1, nh, tq, AUG), lambda bi, i: (bi, 0, i, 0)),
                  pl.BlockSpec((1, nh, tq, dh), lambda bi, i: (bi, 0, i, 0)),
                  pl.BlockSpec((1, 8, tq), lambda bi, i: (bi, 0, i)),
                  pl.BlockSpec((1, s, dh), lambda bi, i: (bi, 0, 0)),
                  pl.BlockSpec((1, s, AUG), lambda bi, i: (bi, 0, 0)),
                  pl.BlockSpec((1, n_ck, dh, ck), lambda bi, i: (bi, 0, 0, 0))],
        out_specs=pl.BlockSpec((1, nh * dh, tq), lambda bi, i: (bi, 0, i)),
        out_shape=jax.ShapeDtypeStruct((b, nh * dh, s), BF16),
        scratch_shapes=[pltpu.VMEM((n_ck, ck, tq), F32), pltpu.VMEM((n_ck, ck, tq), F32)],
        compiler_params=_cparams("parallel", "arbitrary"),
        name="dsa_attention",
    )(q_aug, iq, iw_t, ik, k_aug, v_t)


def _merge_kernel(h_ref, a_ref, b_ref, c_ref, d_ref, wg_ref, bg_ref, wb_ref, wo_ref, g_ref,
                  beta_ref, o_ref, *, alpha):
    h = h_ref[...]
    hb = h.astype(BF16)
    d = h.shape[1]
    acc = jnp.zeros(h.shape, F32)
    for n, br in enumerate((a_ref, b_ref, c_ref, d_ref)):
        gate = _sigmoid(_dot(hb, wg_ref[:, n * d:(n + 1) * d]) + bg_ref[:, n * d:(n + 1) * d])
        acc = acc + gate * _dot(br[...], wb_ref[n])
    mixed = _dot(acc.astype(BF16), wo_ref[...])
    o_ref[...] = _ln_rows(alpha * h + mixed, g_ref[...], beta_ref[...])


def _merge(h2d, branches, w_gate, b_gate, w_branch, w_out, ln_g, ln_b, alpha, tm):
    n, d = h2d.shape
    c = branches[0].shape[1]
    rows = lambda w: pl.BlockSpec((tm, w), lambda i: (i, 0))
    const = lambda shp: pl.BlockSpec(shp, lambda i: (0,) * len(shp))
    return pl.pallas_call(
        functools.partial(_merge_kernel, alpha=alpha),
        grid=(n // tm,),
        in_specs=[rows(d), rows(c), rows(c), rows(c), rows(c), const((d, N_BRANCH * d)),
                  const((1, N_BRANCH * d)), const((N_BRANCH, c, d)), const((d, d)),
                  const((1, d)), const((1, d))],
        out_specs=rows(d),
        out_shape=jax.ShapeDtypeStruct((n, d), F32),
        compiler_params=_cparams("parallel"),
        name="merge",
    )(h2d, *branches, w_gate, b_gate.reshape(1, -1), w_branch, w_out, ln_g.reshape(1, d),
      ln_b.reshape(1, d))


ROUTER_E0 = N_GROUPS


def _routing_weights(logits):
    lane = lax.broadcasted_iota(I32, logits.shape, 1)
    far = jnp.int32(4 * LANES)
    neg = -jnp.inf

    def first_max(x):
        mx = jnp.max(x, axis=-1, keepdims=True)
        idx = jnp.min(jnp.where(x == mx, lane, far), axis=-1, keepdims=True)
        return mx, idx

    lg = jnp.where(lane < N_GROUPS, logits, neg)
    g_max, g_idx = first_max(lg)
    p_group = 1.0 / jnp.sum(jnp.exp(lg - g_max), axis=-1, keepdims=True)
    lo = ROUTER_E0 + EXPERTS_PER_GROUP * g_idx
    le = jnp.where((lane >= lo) & (lane < lo + EXPERTS_PER_GROUP), logits, neg)
    m1, e1 = first_max(le)
    m2, e2 = first_max(jnp.where(lane == e1, neg, le))
    r = jnp.exp(m2 - m1)
    gate1 = p_group * (1.0 / (1.0 + r))
    gate2 = p_group * (r / (1.0 + r))
    return jnp.where(lane == e1, gate1, jnp.where(lane == e2, gate2, 0.0))


def _moe_kernel(h_ref, wr_ref, br_ref, wg_ref, wu_ref, wd_ref, g_ref, beta_ref, o_ref, xb, comb,
                acc, *, alpha):
    e = pl.program_id(1)

    @pl.when(e == 0)
    def _():
        hb = h_ref[...].astype(BF16)
        xb[...] = hb
        comb[...] = _routing_weights(_dot(hb, wr_ref[...]) + br_ref[...])
        acc[...] = jnp.zeros(acc.shape, F32)

    x = xb[...]
    lane = lax.broadcasted_iota(I32, comb.shape, 1)
    coef = jnp.sum(jnp.where(lane == e + ROUTER_E0, comb[...], 0.0), axis=-1, keepdims=True)
    hg = _dot(x, wg_ref[0])
    hu = _dot(x, wu_ref[0])
    hid = (hg * _sigmoid(hg) * hu).astype(BF16)
    acc[...] += coef * _dot(hid, wd_ref[0])

    @pl.when(e == pl.num_programs(1) - 1)
    def _():
        o_ref[...] = _ln_rows(alpha * h_ref[...] + acc[...], g_ref[...], beta_ref[...])


def _moe(h2d, w_router, b_router, wg, wu, wd, ln_g, ln_b, alpha, tm):
    n, d = h2d.shape
    ne, _, f = wg.shape
    rows = pl.BlockSpec((tm, d), lambda i, e: (i, 0))
    const = lambda shp: pl.BlockSpec(shp, lambda i, e: (0,) * len(shp))
    return pl.pallas_call(
        functools.partial(_moe_kernel, alpha=alpha),
        grid=(n // tm, ne),
        in_specs=[rows, const((d, LANES)), const((1, LANES)),
                  pl.BlockSpec((1, d, f), lambda i, e: (e, 0, 0)),
                  pl.BlockSpec((1, d, f), lambda i, e: (e, 0, 0)),
                  pl.BlockSpec((1, f, d), lambda i, e: (e, 0, 0)),
                  const((1, d)), const((1, d))],
        out_specs=rows,
        out_shape=jax.ShapeDtypeStruct((n, d), F32),
        scratch_shapes=[pltpu.VMEM((tm, d), BF16), pltpu.VMEM((tm, LANES), F32),
                        pltpu.VMEM((tm, d), F32)],
        compiler_params=_cparams("parallel", "arbitrary"),
        name="moe",
    )(h2d, w_router, b_router, wg, wu, wd, ln_g.reshape(1, d), ln_b.reshape(1, d))


def _tiles(b, s):
    n = b * s
    pick = lambda total, want: want if total % want == 0 else total
    return dict(
        ln_tm=pick(n, 512), proj_tm=pick(s, 512), local_ts=pick(s, 512),
        fox_tq=pick(s, 256), fox_ck=pick(s, 512), dsa_tq=pick(s, 256), dsa_ck=pick(s, 512),
        merge_tm=pick(n, 512), moe_tm=pick(n, 1024))


def kernel(x, ln_in_g, ln_in_b, w_in, b_forget, b_gate, pool_w, pool_scale, conv_dw, conv_b,
           conv_ln_g, conv_ln_b, w_branch, w_out, ln1_g, ln1_b, router_g, router_g_b, router_e,
           router_e_b, expert_w_gate, expert_w_up, expert_w_down, ln2_g, ln2_b):
    b, s, d = x.shape
    depth = w_in.shape[0]
    alpha = (2.0 * depth) ** 0.25
    t = _tiles(b, s)
    n = b * s

    h = _layer_norm(x.reshape(n, d), ln_in_g, ln_in_b, t["ln_tm"])
    for l in range(depth):
        wp = _pack_proj_weight(w_in[l])
        (pool_v, fq, fk, fv, dq, iq, glu, dk, dv, ik, small) = _projection(
            h.reshape(b, s, d), wp, t["proj_tm"])

        small_t = jnp.swapaxes(small[..., SMALL_FF:SMALL_FF + 2 * N_HEADS], 1, 2)
        f_t = jnp.pad(small_t[:, :N_HEADS], ((0, 0), (0, 8 - N_HEADS), (0, 0)))
        iw_t = jnp.pad(small_t[:, N_HEADS:], ((0, 0), (0, 8 - N_HEADS), (0, 0)))
        bf_col = jnp.pad(b_forget[l], (0, 8 - N_HEADS)).reshape(8, 1)
        c_parts = _forget_cumsum(f_t, bf_col)[:, :, :N_HEADS]

        pool_wbd = jnp.zeros((BRANCH_W, BRANCH_W), F32)
        for g in range(len(POOL_WINDOWS)):
            sl = slice(g * POOL_GROUP, (g + 1) * POOL_GROUP)
            pool_wbd = pool_wbd.at[sl, sl].set(pool_w[l, g])
        br_a, br_d = _local_mixers(pool_v, glu, pool_wbd.astype(BF16), pool_scale[l], conv_dw[l],
                                   conv_b[l], conv_ln_g[l], conv_ln_b[l], t["local_ts"])
        br_b = _fox_attention(fq, fk, fv, c_parts, t["fox_tq"], t["fox_ck"])
        br_c = _dsa_attention(dq, iq, iw_t, ik, dk, dv, t["dsa_tq"], t["dsa_ck"])
        br_b, br_c = jnp.swapaxes(br_b, 1, 2), jnp.swapaxes(br_c, 1, 2)

        branches = [br.reshape(n, BRANCH_W) for br in (br_a, br_b, br_c, br_d)]
        h = _merge(h, branches, w_in[l][:, GATE_OFF:].astype(BF16), b_gate[l],
                   w_branch[l].astype(BF16), w_out[l].astype(BF16), ln1_g[l], ln1_b[l], alpha,
                   t["merge_tm"])

        w_router = jnp.concatenate(
            [router_g[l], router_e[l], jnp.zeros((d, LANES - N_GROUPS - N_EXPERTS), F32)],
            axis=1).astype(BF16)
        b_router = jnp.concatenate(
            [router_g_b[l], router_e_b[l], jnp.zeros((LANES - N_GROUPS - N_EXPERTS,), F32)]
        ).reshape(1, LANES)
        h = _moe(h, w_router, b_router, expert_w_gate[l].astype(BF16), expert_w_up[l].astype(BF16),
                 expert_w_down[l].astype(BF16), ln2_g[l], ln2_b[l], alpha, t["moe_tm"])
    return h.reshape(b, s, d)
```

```python
import functools

import jax
import jax.numpy as jnp
from jax import lax
from jax.experimental import pallas as pl
from jax.experimental.pallas import tpu as pltpu

F32 = jnp.float32
BF16 = jnp.bfloat16
I32 = jnp.int32

N_BRANCH = 4
HEAD_DIM = 64
N_HEADS = 4
BRANCH_W = N_HEADS * HEAD_DIM
POOL_WINDOWS = (2, 4, 8, 16)
POOL_GROUP = BRANCH_W // len(POOL_WINDOWS)
CONV_W = 31
DSA_TOPK = 256
N_GROUPS = 4
EXPERTS_PER_GROUP = 4
N_EXPERTS = N_GROUPS * EXPERTS_PER_GROUP
LN_EPS = 1e-5
NEG_INF = -1e30
ATTN_SCALE = HEAD_DIM ** -0.5

LANES = 128
HALO = 32
VMEM_LIMIT = 56 * 1024 * 1024

_OFF = {}
_o = 0
for _name, _n in (("pool", BRANCH_W), ("fq", BRANCH_W), ("fk", BRANCH_W), ("fv", BRANCH_W),
                  ("ff", N_HEADS), ("dq", BRANCH_W), ("dk", HEAD_DIM), ("dv", HEAD_DIM),
                  ("iq", BRANCH_W), ("ik", HEAD_DIM), ("iw", N_HEADS), ("conv", 2 * BRANCH_W)):
    _OFF[_name] = (_o, _o + _n)
    _o += _n
GATE_OFF = _o
SMALL_FF = HEAD_DIM
SMALL_IW = HEAD_DIM + N_HEADS


def _cparams(*sem):
    return pltpu.CompilerParams(dimension_semantics=sem, vmem_limit_bytes=VMEM_LIMIT)


def _ln_rows(x, g, b):
    mu = jnp.mean(x, axis=-1, keepdims=True)
    xc = x - mu
    var = jnp.mean(xc * xc, axis=-1, keepdims=True)
    return xc * lax.rsqrt(var + LN_EPS) * g + b


def _sigmoid(x):
    return 1.0 / (1.0 + jnp.exp(-x))


def _dot(a, b):
    return jnp.dot(a, b, preferred_element_type=F32)


def _dot_nt(a, b):
    return lax.dot_general(a, b, (((1,), (1,)), ((), ())), preferred_element_type=F32)


def _ln_kernel(x_ref, g_ref, b_ref, o_ref):
    o_ref[...] = _ln_rows(x_ref[...], g_ref[...], b_ref[...])


def _layer_norm(x2d, g, b, tm):
    n, d = x2d.shape
    return pl.pallas_call(
        _ln_kernel,
        grid=(n // tm,),
        in_specs=[pl.BlockSpec((tm, d), lambda i: (i, 0)),
                  pl.BlockSpec((1, d), lambda i: (0, 0)),
                  pl.BlockSpec((1, d), lambda i: (0, 0))],
        out_specs=pl.BlockSpec((tm, d), lambda i: (i, 0)),
        out_shape=jax.ShapeDtypeStruct((n, d), F32),
        compiler_params=_cparams("parallel"),
        name="ln_in",
    )(x2d, g.reshape(1, d), b.reshape(1, d))


_PK = {"pool": 0, "fq": 256, "fk": 512, "fv": 768, "dq": 1024, "iq": 1280, "conv": 1536,
       "dkv": 2048, "small": 2176}
PK_COLS = 2304


def _pack_proj_weight(w):
    d = w.shape[0]
    sl = lambda k: w[:, _OFF[k][0]:_OFF[k][1]]
    small = jnp.concatenate([sl("ik"), sl("ff"), sl("iw"),
                             jnp.zeros((d, LANES - HEAD_DIM - 2 * N_HEADS), w.dtype)], axis=1)
    packed = jnp.concatenate([sl("pool"), sl("fq") * ATTN_SCALE, sl("fk"), sl("fv"),
                              sl("dq") * ATTN_SCALE, sl("iq"), sl("conv"), sl("dk"), sl("dv"),
                              small], axis=1)
    return packed.astype(BF16)


def _proj_kernel(h_ref, w_ref, pool_o, fq_o, fk_o, fv_o, dq_o, iq_o, glu_o, dk_o, dv_o, ik_o,
                 small_o):
    hb = h_ref[0].astype(BF16)

    def piece(name, width):
        return _dot(hb, w_ref[:, _PK[name]:_PK[name] + width])

    pool_o[0] = piece("pool", BRANCH_W)
    for name, out in (("fq", fq_o), ("fk", fk_o), ("fv", fv_o), ("dq", dq_o), ("iq", iq_o)):
        r = piece(name, BRANCH_W).astype(BF16)
        for h in range(N_HEADS):
            out[0, h] = r[:, h * HEAD_DIM:(h + 1) * HEAD_DIM]
    a = piece("conv", 2 * BRANCH_W)
    glu_o[0] = a[:, :BRANCH_W] * _sigmoid(a[:, BRANCH_W:])
    kv = piece("dkv", 2 * HEAD_DIM).astype(BF16)
    dk_o[0] = kv[:, :HEAD_DIM]
    dv_o[0] = kv[:, HEAD_DIM:]
    sm = piece("small", LANES)
    small_o[0] = sm
    ik_o[0] = sm[:, :HEAD_DIM].astype(BF16)


def _projection(h, wp, tm):
    b, s, d = h.shape
    heads = lambda: pl.BlockSpec((1, N_HEADS, tm, HEAD_DIM), lambda bi, i: (bi, 0, i, 0))
    rows = lambda w: pl.BlockSpec((1, tm, w), lambda bi, i: (bi, i, 0))
    head_shape = jax.ShapeDtypeStruct((b, N_HEADS, s, HEAD_DIM), BF16)
    return pl.pallas_call(
        _proj_kernel,
        grid=(b, s // tm),
        in_specs=[rows(d), pl.BlockSpec((d, PK_COLS), lambda bi, i: (0, 0))],
        out_specs=[rows(BRANCH_W), heads(), heads(), heads(), heads(), heads(), rows(BRANCH_W),
                   rows(HEAD_DIM), rows(HEAD_DIM), rows(HEAD_DIM), rows(LANES)],
        out_shape=[jax.ShapeDtypeStruct((b, s, BRANCH_W), F32), head_shape, head_shape, head_shape,
                   head_shape, head_shape, jax.ShapeDtypeStruct((b, s, BRANCH_W), F32),
                   jax.ShapeDtypeStruct((b, s, HEAD_DIM), BF16),
                   jax.ShapeDtypeStruct((b, s, HEAD_DIM), BF16),
                   jax.ShapeDtypeStruct((b, s, HEAD_DIM), BF16),
                   jax.ShapeDtypeStruct((b, s, LANES), F32)],
        compiler_params=_cparams("parallel", "parallel"),
        name="proj",
    )(h, wp)


def _forget_cumsum_kernel(f_ref, b_ref, o_ref):
    s = f_ref.shape[2]
    x = f_ref[0] + b_ref[...]
    ls = jnp.minimum(x, 0.0) - jnp.log(1.0 + jnp.exp(-jnp.abs(x)))
    r = lax.broadcasted_iota(I32, (LANES, LANES), 0)
    c = lax.broadcasted_iota(I32, (LANES, LANES), 1)
    upper = jnp.where(r <= c, 1.0, 0.0).astype(F32)
    carry = jnp.zeros((x.shape[0], 1), F32)
    for j in range(s // LANES):
        blk = jnp.dot(ls[:, j * LANES:(j + 1) * LANES], upper, preferred_element_type=F32,
                      precision=lax.Precision.HIGHEST) + carry
        carry = blk[:, LANES - 1:LANES]
        rest = blk
        for piece in range(3):
            part = rest.astype(BF16)
            o_ref[0, piece, :, j * LANES:(j + 1) * LANES] = part
            rest = rest - part.astype(F32)


def _forget_cumsum(f_t, b_col):
    b, r, s = f_t.shape
    return pl.pallas_call(
        _forget_cumsum_kernel,
        grid=(b,),
        in_specs=[pl.BlockSpec((1, r, s), lambda bi: (bi, 0, 0)),
                  pl.BlockSpec((r, 1), lambda bi: (0, 0))],
        out_specs=pl.BlockSpec((1, 3, r, s), lambda bi: (bi, 0, 0, 0)),
        out_shape=jax.ShapeDtypeStruct((b, 3, r, s), BF16),
        compiler_params=_cparams("parallel"),
        name="forget_cumsum",
    )(f_t, b_col)


def _local_kernel(pv_ref, pvh_ref, gl_ref, glh_ref, pw_ref, ps_ref, dw_ref, cb_ref, g_ref, b_ref,
                  a_o, d_o, pbuf, cbuf, *, ts):
    i = pl.program_id(1)
    first = i == 0
    pbuf[0:HALO, :] = jnp.where(first, 0.0, pvh_ref[0])
    pbuf[HALO:HALO + ts, :] = pv_ref[0]
    cbuf[0:HALO, :] = jnp.where(first, 0.0, glh_ref[0])
    cbuf[HALO:HALO + ts, :] = gl_ref[0]

    lane = lax.broadcasted_iota(I32, (ts, BRANCH_W), 1)
    pos = (i * ts + 1 + lax.broadcasted_iota(I32, (ts, 1), 0)).astype(F32)
    v = pbuf[HALO:HALO + ts, :]
    run = v
    pooled = jnp.zeros((ts, BRANCH_W), F32)
    shift = 1
    for g, w in enumerate(POOL_WINDOWS):
        while shift < w:
            run = run + pbuf[HALO - shift:HALO - shift + ts, :]
            shift += 1
        in_group = (lane >= g * POOL_GROUP) & (lane < (g + 1) * POOL_GROUP)
        pooled = jnp.where(in_group, run / jnp.minimum(pos, float(w)), pooled)
    pooled = pooled - v
    a_o[0] = (_dot(pooled.astype(BF16), pw_ref[...]) * ps_ref[...]).astype(BF16)

    y = jnp.zeros((ts, BRANCH_W), F32) + cb_ref[...]
    for j in range(CONV_W):
        back = CONV_W - 1 - j
        y = y + cbuf[HALO - back:HALO - back + ts, :] * dw_ref[j:j + 1, :]
    z = _ln_rows(y, g_ref[...], b_ref[...])
    d_o[0] = (z * _sigmoid(z)).astype(BF16)


def _local_mixers(pool_v, glu, pool_wbd, pool_scale, conv_dw, conv_b, ln_g, ln_b, ts):
    b, s, c = pool_v.shape
    cur = pl.BlockSpec((1, ts, c), lambda bi, i: (bi, i, 0))
    halo = pl.BlockSpec((1, HALO, c), lambda bi, i: (bi, jnp.maximum(i * (ts // HALO) - 1, 0), 0))
    full2 = lambda shp: pl.BlockSpec(shp, lambda bi, i: (0, 0))
    out_shape = jax.ShapeDtypeStruct((b, s, c), BF16)
    return pl.pallas_call(
        functools.partial(_local_kernel, ts=ts),
        grid=(b, s // ts),
        in_specs=[cur, halo, cur, halo, full2((c, c)), full2((1, c)), full2((CONV_W, c)),
                  full2((1, c)), full2((1, c)), full2((1, c))],
        out_specs=[cur, cur],
        out_shape=[out_shape, out_shape],
        scratch_shapes=[pltpu.VMEM((HALO + ts, c), F32), pltpu.VMEM((HALO + ts, c), F32)],
        compiler_params=_cparams("parallel", "parallel"),
        name="local_mixers",
    )(pool_v, pool_v, glu, glu, pool_wbd, pool_scale.reshape(1, c), conv_dw, conv_b.reshape(1, c),
      ln_g.reshape(1, c), ln_b.reshape(1, c))


AUG = 2 * HEAD_DIM


SUBLANES = 8
FOLD_CHAINS = 4


def _fold_rows(x, op):
    slabs = [x[i:i + SUBLANES] for i in range(0, x.shape[0], SUBLANES)]
    accs = slabs[:FOLD_CHAINS]
    for i, slab in enumerate(slabs[FOLD_CHAINS:]):
        accs[i % FOLD_CHAINS] = op(accs[i % FOLD_CHAINS], slab)
    while len(accs) > 1:
        accs = [op(accs[i], accs[i + 1]) for i in range(0, len(accs) - 1, 2)] + (
            [accs[-1]] if len(accs) % 2 else [])
    return accs[0]


def _softmax_chunk(s, m, l):
    m_new = jnp.maximum(m, jnp.max(_fold_rows(s, jnp.maximum), axis=0, keepdims=True))
    alpha = jnp.exp(m - m_new)
    p = jnp.exp(s - m_new)
    l_new = alpha * l + jnp.sum(_fold_rows(p, jnp.add), axis=0, keepdims=True)
    return m_new, l_new, alpha, p.astype(BF16)


def _causal_split(qi, tq, ck):
    return (qi * tq + 1) // ck, ((qi + 1) * tq + ck - 1) // ck


CHUNKS_PER_STEP = 2


def _attention_loops(n_plain, n_chunks, tq, scores, edit_last, values):
    def step(j0, carry, n, edit):
        ss = [scores(j0 + c) for c in range(n)]
        carry = list(carry)
        for c in range(n):
            stats, ps = [], []
            for h in range(N_HEADS):
                s = edit_last(j0 + c, h, ss[c][h]) if edit else ss[c][h]
                m_new, l_new, alpha, p = _softmax_chunk(s, carry[h][0], carry[h][1])
                stats.append((m_new, l_new, alpha))
                ps.append(p)
            for h in range(N_HEADS):
                m_new, l_new, alpha = stats[h]
                carry[h] = (m_new, l_new, alpha * carry[h][2] + _dot(values(j0 + c, h), ps[h]))
        return tuple(carry)

    carry = tuple((jnp.full((1, tq), NEG_INF, F32), jnp.zeros((1, tq), F32),
                   jnp.zeros((HEAD_DIM, tq), F32)) for _ in range(N_HEADS))
    n_multi = n_plain // CHUNKS_PER_STEP
    carry = lax.fori_loop(
        0, n_multi, lambda i, c: step(i * CHUNKS_PER_STEP, c, CHUNKS_PER_STEP, False), carry)
    carry = lax.fori_loop(n_multi * CHUNKS_PER_STEP, n_plain,
                          lambda j, c: step(j, c, 1, False), carry)
    if edit_last is None:
        return carry
    return lax.fori_loop(n_plain, n_chunks, lambda j, c: step(j, c, 1, True), carry)


def _fox_kernel(q_ref, k_ref, vt_ref, o_ref, *, tq, ck):
    qi = pl.program_id(1)
    n_full, n_chunks = _causal_split(qi, tq, ck)
    qpos = qi * tq + lax.broadcasted_iota(I32, (1, tq), 1)
    kiota = lax.broadcasted_iota(I32, (ck, 1), 0)
    qs = [q_ref[0, h] for h in range(N_HEADS)]

    def scores(j):
        start = pl.multiple_of(j * ck, ck)
        return [_dot_nt(k_ref[0, h, pl.ds(start, ck), :], qs[h]) for h in range(N_HEADS)]

    def causal(j, h, s):
        return jnp.where(j * ck + kiota <= qpos, s, NEG_INF)

    def values(j, h):
        return vt_ref[0, h, j]

    carry = _attention_loops(n_full, n_chunks, tq, scores, causal, values)
    for h in range(N_HEADS):
        _, l, acc = carry[h]
        o_ref[0, h * HEAD_DIM:(h + 1) * HEAD_DIM, :] = (acc / l).astype(BF16)


def _fox_attention(fq, fk, fv, c_parts, tq, ck):
    b, nh, s, dh = fq.shape
    n_ck = s // ck
    pad = jnp.zeros((b, nh, s, AUG - dh - 3), BF16)
    q_aug = jnp.concatenate([fq, jnp.full((b, nh, s, 3), -1.0, BF16), pad], axis=-1)
    k_aug = jnp.concatenate([fk, jnp.moveaxis(c_parts, 1, 3), pad], axis=-1)
    v_t = jnp.swapaxes(fv.reshape(b, nh, n_ck, ck, dh), 3, 4)
    return pl.pallas_call(
        functools.partial(_fox_kernel, tq=tq, ck=ck),
        grid=(b, s // tq),
        in_specs=[pl.BlockSpec((1, nh, tq, AUG), lambda bi, i: (bi, 0, i, 0)),
                  pl.BlockSpec((1, nh, s, AUG), lambda bi, i: (bi, 0, 0, 0)),
                  pl.BlockSpec((1, nh, n_ck, dh, ck), lambda bi, i: (bi, 0, 0, 0, 0))],
        out_specs=pl.BlockSpec((1, nh * dh, tq), lambda bi, i: (bi, 0, i)),
        out_shape=jax.ShapeDtypeStruct((b, nh * dh, s), BF16),
        compiler_params=_cparams("parallel", "arbitrary"),
        name="fox_attention",
    )(q_aug, k_aug, v_t)


def _key_to_f32(key):
    bits = key ^ (lax.shift_right_arithmetic(key, 31) & 0x7FFFFFFF)
    return lax.bitcast_convert_type(bits, F32)


def _f32_to_key(x):
    bits = lax.bitcast_convert_type(x, I32)
    return bits ^ (lax.shift_right_arithmetic(bits, 31) & 0x7FFFFFFF)


KEY_POS_INF = 0x7F800000
KEY_NEG_INF = 0x007FFFFF - 2 ** 31
PROBE_STEP = 1 << 24
MAX_PROBES = 36


def _dsa_kernel(dq_ref, iq_ref, iwt_ref, ik_ref, dk_ref, dvt_ref, o_ref, score_buf, bias_buf, *,
                tq, ck, topk):
    qi = pl.program_id(1)
    n_full, n_chunks = _causal_split(qi, tq, ck)
    qpos = qi * tq + lax.broadcasted_iota(I32, (1, tq), 1)
    kiota = lax.broadcasted_iota(I32, (ck, 1), 0)
    w_idx = iwt_ref[0]
    kf = float(topk)

    def count(pred):
        def cb(j, acc):
            return acc + _fold_rows(jnp.where(pred(score_buf[j]), 1.0, 0.0), jnp.add)
        acc = lax.fori_loop(0, n_chunks, cb, jnp.zeros((SUBLANES, tq), F32))
        return jnp.sum(acc, axis=0, keepdims=True)

    def score_body(j, _, masked):
        start = pl.multiple_of(j * ck, ck)
        ikc = ik_ref[0, pl.ds(start, ck), :]
        sc = None
        for h in range(N_HEADS):
            term = jnp.maximum(_dot_nt(ikc, iq_ref[0, h]), 0.0) * w_idx[h:h + 1, :]
            sc = term if sc is None else sc + term
        if masked:
            sc = jnp.where(start + kiota <= qpos, sc, NEG_INF)
        score_buf[j] = sc
        return 0

    lax.fori_loop(0, n_full, functools.partial(score_body, masked=False), 0)
    lax.fori_loop(n_full, n_chunks, functools.partial(score_body, masked=True), 0)

    def row_max():
        def cb(j, acc):
            return jnp.maximum(acc, _fold_rows(score_buf[j], jnp.maximum))
        acc = lax.fori_loop(0, n_chunks, cb, jnp.full((SUBLANES, tq), -jnp.inf, F32))
        return jnp.max(acc, axis=0, keepdims=True)

    top_key = _f32_to_key(row_max())
    c_pos = count(lambda s: s > 0.0)
    c_nn = count(lambda s: s >= 0.0)
    pos_side, zero_tie = c_pos >= kf, c_nn >= kf
    n_tot = (n_chunks * ck).astype(F32)
    lo = jnp.where(pos_side, 1, jnp.where(zero_tie, 0, KEY_NEG_INF)).astype(I32)
    hi = jnp.where(pos_side, KEY_POS_INF, jnp.where(zero_tie, 1, 0)).astype(I32)
    c_lo = jnp.where(pos_side, c_pos, jnp.where(zero_tie, c_nn, n_tot))

    def unsettled(lo, hi, c_lo):
        return (c_lo != kf) & (hi > lo + 1)

    def probe(state):
        it, lo, hi, c_lo = state
        mid = (lo >> 1) + (hi >> 1) + (lo & hi & 1)
        near_top = jnp.where(hi > 0, jnp.maximum(mid, hi - PROBE_STEP), mid)
        cand = jnp.where(it == 0, top_key, jnp.where(it == 1, near_top, mid))
        cand = jnp.minimum(jnp.maximum(cand, lo + 1), hi - 1)
        cand_f = _key_to_f32(cand)
        c = count(lambda s: s >= cand_f)
        live = unsettled(lo, hi, c_lo)
        up = live & (c >= kf)
        down = live & (c < kf)
        return (it + 1, jnp.where(up, cand, lo), jnp.where(down, cand, hi),
                jnp.where(up, c, c_lo))

    def searching(state):
        it, lo, hi, c_lo = state
        left = jnp.max(jnp.where(unsettled(lo, hi, c_lo), 1.0, 0.0))
        return (it < MAX_PROBES) & (left > 0.0)

    _, lo, _, _ = lax.while_loop(searching, lambda st: probe(probe(st)),
                                 (jnp.int32(0), lo, hi, c_lo))
    thr = _key_to_f32(lo)

    need = kf - count(lambda s: s > thr)
    r_i = lax.broadcasted_iota(I32, (ck, ck), 0)
    c_i = lax.broadcasted_iota(I32, (ck, ck), 1)
    lower = jnp.where(c_i <= r_i, 1.0, 0.0).astype(BF16)

    def bias_body(j, seen, masked):
        start = pl.multiple_of(j * ck, ck)
        sc = score_buf[j]
        tie = sc == thr
        rank = _dot(lower, jnp.where(tie, 1.0, 0.0).astype(BF16)) + seen
        keep = jnp.where(sc > thr, 0.0, jnp.where(tie, jnp.where(rank <= need, 0.0, NEG_INF),
                                                  NEG_INF))
        if masked:
            keep = jnp.where(start + kiota <= qpos, keep, NEG_INF)
        bias_buf[j] = keep
        return rank[ck - 1:ck, :]

    seen = lax.fori_loop(0, n_full, functools.partial(bias_body, masked=False),
                         jnp.zeros((1, tq), F32))
    lax.fori_loop(n_full, n_chunks, functools.partial(bias_body, masked=True), seen)

    qs = [dq_ref[0, h] for h in range(N_HEADS)]

    def scores(j):
        start = pl.multiple_of(j * ck, ck)
        kc = dk_ref[0, pl.ds(start, ck), :]
        bias = bias_buf[j]
        return [_dot_nt(kc, qs[h]) + bias for h in range(N_HEADS)]

    carry = _attention_loops(n_chunks, n_chunks, tq, scores, None, lambda j, h: dvt_ref[0, j])
    for h in range(N_HEADS):
        _, l, acc = carry[h]
        o_ref[0, h * HEAD_DIM:(h + 1) * HEAD_DIM, :] = (acc / l).astype(BF16)


def _alibi_columns(s):
    pos = jnp.arange(s, dtype=I32)
    hi, lo = (pos // 64).astype(F32), (pos % 64).astype(F32)
    one = jnp.ones((s,), F32)
    k_cols = jnp.stack([hi, lo, one, one], axis=-1)
    q_cols = []
    for h in range(N_HEADS):
        slope = 2.0 ** (-8.0 * (h + 1) / N_HEADS)
        q_cols.append(jnp.stack([64.0 * slope * one, slope * one, -64.0 * slope * hi, -slope * lo],
                                axis=-1))
    return k_cols.astype(BF16), jnp.stack(q_cols, axis=0).astype(BF16)


def _dsa_attention(dq, iq, iw_t, ik, dk, dv, tq, ck):
    b, nh, s, dh = dq.shape
    assert s <= 64 * 64, "position columns hold two base-64 digits"
    topk = min(DSA_TOPK, s // 4)
    assert ck >= topk, "every query must see at least topk (possibly masked) candidates"
    n_ck = s // ck
    k_cols, q_cols = _alibi_columns(s)
    q_aug = jnp.concatenate([dq, jnp.broadcast_to(q_cols[None], (b, nh, s, 4)),
                             jnp.zeros((b, nh, s, AUG - dh - 4), BF16)], axis=-1)
    k_aug = jnp.concatenate([dk, jnp.broadcast_to(k_cols[None], (b, s, 4)),
                             jnp.zeros((b, s, AUG - dh - 4), BF16)], axis=-1)
    v_t = jnp.swapaxes(dv.reshape(b, n_ck, ck, dh), 2, 3)
    return pl.pallas_call(
        functools.partial(_dsa_kernel, tq=tq, ck=ck, topk=topk),
        grid=(b, s // tq),
        in_specs=[pl.BlockSpec((1, nh, tq, AUG), lambda bi, i: (bi, 0, i, 0)),
                  pl.BlockSpec((1, nh, tq, dh), lambda bi, i: (bi, 0, i, 0)),
                  pl.BlockSpec((1, 8, tq), lambda bi, i: (bi, 0, i)),
                  pl.BlockSpec((1, s, dh), lambda bi, i: (bi, 0, 0)),
                  pl.BlockSpec((1, s, AUG), lambda bi, i: (bi, 0, 0)),
                  pl.BlockSpec((1, n_ck, dh, ck), lambda bi, i: (bi, 0, 0, 0))],
        out_specs=pl.BlockSpec((1, nh * dh, tq), lambda bi, i: (bi, 0, i)),
        out_shape=jax.ShapeDtypeStruct((b, nh * dh, s), BF16),
        scratch_shapes=[pltpu.VMEM((n_ck, ck, tq), F32), pltpu.VMEM((n_ck, ck, tq), F32)],
        compiler_params=_cparams("parallel", "arbitrary"),
        name="dsa_attention",
    )(q_aug, iq, iw_t, ik, k_aug, v_t)


def _merge_kernel(h_ref, a_ref, b_ref, c_ref, d_ref, wg_ref, bg_ref, wb_ref, wo_ref, g_ref,
                  beta_ref, o_ref, *, alpha):
    h = h_ref[...]
    hb = h.astype(BF16)
    d = h.shape[1]
    acc = jnp.zeros(h.shape, F32)
    for n, br in enumerate((a_ref, b_ref, c_ref, d_ref)):
        gate = _sigmoid(_dot(hb, wg_ref[:, n * d:(n + 1) * d]) + bg_ref[:, n * d:(n + 1) * d])
        acc = acc + gate * _dot(br[...], wb_ref[n])
    mixed = _dot(acc.astype(BF16), wo_ref[...])
    o_ref[...] = _ln_rows(alpha * h + mixed, g_ref[...], beta_ref[...])


def _merge(h2d, branches, w_gate, b_gate, w_branch, w_out, ln_g, ln_b, alpha, tm):
    n, d = h2d.shape
    c = branches[0].shape[1]
    rows = lambda w: pl.BlockSpec((tm, w), lambda i: (i, 0))
    const = lambda shp: pl.BlockSpec(shp, lambda i: (0,) * len(shp))
    return pl.pallas_call(
        functools.partial(_merge_kernel, alpha=alpha),
        grid=(n // tm,),
        in_specs=[rows(d), rows(c), rows(c), rows(c), rows(c), const((d, N_BRANCH * d)),
                  const((1, N_BRANCH * d)), const((N_BRANCH, c, d)), const((d, d)),
                  const((1, d)), const((1, d))],
        out_specs=rows(d),
        out_shape=jax.ShapeDtypeStruct((n, d), F32),
        compiler_params=_cparams("parallel"),
        name="merge",
    )(h2d, *branches, w_gate, b_gate.reshape(1, -1), w_branch, w_out, ln_g.reshape(1, d),
      ln_b.reshape(1, d))


ROUTER_E0 = N_GROUPS


def _routing_weights(logits):
    lane = lax.broadcasted_iota(I32, logits.shape, 1)
    far = jnp.int32(4 * LANES)
    neg = -jnp.inf

    def first_max(x):
        mx = jnp.max(x, axis=-1, keepdims=True)
        idx = jnp.min(jnp.where(x == mx, lane, far), axis=-1, keepdims=True)
        return mx, idx

    lg = jnp.where(lane < N_GROUPS, logits, neg)
    g_max, g_idx = first_max(lg)
    p_group = 1.0 / jnp.sum(jnp.exp(lg - g_max), axis=-1, keepdims=True)
    lo = ROUTER_E0 + EXPERTS_PER_GROUP * g_idx
    le = jnp.where((lane >= lo) & (lane < lo + EXPERTS_PER_GROUP), logits, neg)
    m1, e1 = first_max(le)
    m2, e2 = first_max(jnp.where(lane == e1, neg, le))
    r = jnp.exp(m2 - m1)
    gate1 = p_group * (1.0 / (1.0 + r))
    gate2 = p_group * (r / (1.0 + r))
    return jnp.where(lane == e1, gate1, jnp.where(lane == e2, gate2, 0.0))


def _moe_kernel(h_ref, wr_ref, br_ref, wg_ref, wu_ref, wd_ref, g_ref, beta_ref, o_ref, xb, comb,
                acc, *, alpha):
    e = pl.program_id(1)

    @pl.when(e == 0)
    def _():
        hb = h_ref[...].astype(BF16)
        xb[...] = hb
        comb[...] = _routing_weights(_dot(hb, wr_ref[...]) + br_ref[...])
        acc[...] = jnp.zeros(acc.shape, F32)

    x = xb[...]
    lane = lax.broadcasted_iota(I32, comb.shape, 1)
    coef = jnp.sum(jnp.where(lane == e + ROUTER_E0, comb[...], 0.0), axis=-1, keepdims=True)
    hg = _dot(x, wg_ref[0])
    hu = _dot(x, wu_ref[0])
    hid = (hg * _sigmoid(hg) * hu).astype(BF16)
    acc[...] += coef * _dot(hid, wd_ref[0])

    @pl.when(e == pl.num_programs(1) - 1)
    def _():
        o_ref[...] = _ln_rows(alpha * h_ref[...] + acc[...], g_ref[...], beta_ref[...])


def _moe(h2d, w_router, b_router, wg, wu, wd, ln_g, ln_b, alpha, tm):
    n, d = h2d.shape
    ne, _, f = wg.shape
    rows = pl.BlockSpec((tm, d), lambda i, e: (i, 0))
    const = lambda shp: pl.BlockSpec(shp, lambda i, e: (0,) * len(shp))
    return pl.pallas_call(
        functools.partial(_moe_kernel, alpha=alpha),
        grid=(n // tm, ne),
        in_specs=[rows, const((d, LANES)), const((1, LANES)),
                  pl.BlockSpec((1, d, f), lambda i, e: (e, 0, 0)),
                  pl.BlockSpec((1, d, f), lambda i, e: (e, 0, 0)),
                  pl.BlockSpec((1, f, d), lambda i, e: (e, 0, 0)),
                  const((1, d)), const((1, d))],
        out_specs=rows,
        out_shape=jax.ShapeDtypeStruct((n, d), F32),
        scratch_shapes=[pltpu.VMEM((tm, d), BF16), pltpu.VMEM((tm, LANES), F32),
                        pltpu.VMEM((tm, d), F32)],
        compiler_params=_cparams("parallel", "arbitrary"),
        name="moe",
    )(h2d, w_router, b_router, wg, wu, wd, ln_g.reshape(1, d), ln_b.reshape(1, d))


def _tiles(b, s):
    n = b * s
    pick = lambda total, want: want if total % want == 0 else total
    return dict(
        ln_tm=pick(n, 512), proj_tm=pick(s, 512), local_ts=pick(s, 512),
        fox_tq=pick(s, 256), fox_ck=pick(s, 512), dsa_tq=pick(s, 256), dsa_ck=pick(s, 512),
        merge_tm=pick(n, 512), moe_tm=pick(n, 1024))


def kernel(x, ln_in_g, ln_in_b, w_in, b_forget, b_gate, pool_w, pool_scale, conv_dw, conv_b,
           conv_ln_g, conv_ln_b, w_branch, w_out, ln1_g, ln1_b, router_g, router_g_b, router_e,
           router_e_b, expert_w_gate, expert_w_up, expert_w_down, ln2_g, ln2_b):
    b, s, d = x.shape
    depth = w_in.shape[0]
    alpha = (2.0 * depth) ** 0.25
    t = _tiles(b, s)
    n = b * s

    h = _layer_norm(x.reshape(n, d), ln_in_g, ln_in_b, t["ln_tm"])
    for l in range(depth):
        wp = _pack_proj_weight(w_in[l])
        (pool_v, fq, fk, fv, dq, iq, glu, dk, dv, ik, small) = _projection(
            h.reshape(b, s, d), wp, t["proj_tm"])

        small_t = jnp.swapaxes(small[..., SMALL_FF:SMALL_FF + 2 * N_HEADS], 1, 2)
        f_t = jnp.pad(small_t[:, :N_HEADS], ((0, 0), (0, 8 - N_HEADS), (0, 0)))
        iw_t = jnp.pad(small_t[:, N_HEADS:], ((0, 0), (0, 8 - N_HEADS), (0, 0)))
        bf_col = jnp.pad(b_forget[l], (0, 8 - N_HEADS)).reshape(8, 1)
        c_parts = _forget_cumsum(f_t, bf_col)[:, :, :N_HEADS]

        pool_wbd = jnp.zeros((BRANCH_W, BRANCH_W), F32)
        for g in range(len(POOL_WINDOWS)):
            sl = slice(g * POOL_GROUP, (g + 1) * POOL_GROUP)
            pool_wbd = pool_wbd.at[sl, sl].set(pool_w[l, g])
        br_a, br_d = _local_mixers(pool_v, glu, pool_wbd.astype(BF16), pool_scale[l], conv_dw[l],
                                   conv_b[l], conv_ln_g[l], conv_ln_b[l], t["local_ts"])
        br_b = _fox_attention(fq, fk, fv, c_parts, t["fox_tq"], t["fox_ck"])
        br_c = _dsa_attention(dq, iq, iw_t, ik, dk, dv, t["dsa_tq"], t["dsa_ck"])
        br_b, br_c = jnp.swapaxes(br_b, 1, 2), jnp.swapaxes(br_c, 1, 2)

        branches = [br.reshape(n, BRANCH_W) for br in (br_a, br_b, br_c, br_d)]
        h = _merge(h, branches, w_in[l][:, GATE_OFF:].astype(BF16), b_gate[l],
                   w_branch[l].astype(BF16), w_out[l].astype(BF16), ln1_g[l], ln1_b[l], alpha,
                   t["merge_tm"])

        w_router = jnp.concatenate(
            [router_g[l], router_e[l], jnp.zeros((d, LANES - N_GROUPS - N_EXPERTS), F32)],
            axis=1).astype(BF16)
        b_router = jnp.concatenate(
            [router_g_b[l], router_e_b[l], jnp.zeros((LANES - N_GROUPS - N_EXPERTS,), F32)]
        ).reshape(1, LANES)
        h = _moe(h, w_router, b_router, expert_w_gate[l].astype(BF16), expert_w_up[l].astype(BF16),
                 expert_w_down[l].astype(BF16), ln2_g[l], ln2_b[l], alpha, t["moe_tm"])
    return h.reshape(b, s, d)
```

```python
import functools

import jax
import jax.numpy as jnp
from jax import lax
from jax.experimental import pallas as pl
from jax.experimental.pallas import tpu as pltpu

F32 = jnp.float32
BF16 = jnp.bfloat16
I32 = jnp.int32

N_BRANCH = 4
HEAD_DIM = 64
N_HEADS = 4
BRANCH_W = N_HEADS * HEAD_DIM
POOL_WINDOWS = (2, 4, 8, 16)
POOL_GROUP = BRANCH_W // len(POOL_WINDOWS)
CONV_W = 31
DSA_TOPK = 256
N_GROUPS = 4
EXPERTS_PER_GROUP = 4
N_EXPERTS = N_GROUPS * EXPERTS_PER_GROUP
LN_EPS = 1e-5
NEG_INF = -1e30
ATTN_SCALE = HEAD_DIM ** -0.5

LANES = 128
HALO = 32
VMEM_LIMIT = 56 * 1024 * 1024
MOE_VMEM_LIMIT = 60 * 1024 * 1024

_OFF = {}
_o = 0
for _name, _n in (("pool", BRANCH_W), ("fq", BRANCH_W), ("fk", BRANCH_W), ("fv", BRANCH_W),
                  ("ff", N_HEADS), ("dq", BRANCH_W), ("dk", HEAD_DIM), ("dv", HEAD_DIM),
                  ("iq", BRANCH_W), ("ik", HEAD_DIM), ("iw", N_HEADS), ("conv", 2 * BRANCH_W)):
    _OFF[_name] = (_o, _o + _n)
    _o += _n
GATE_OFF = _o
SMALL_FF = HEAD_DIM
SMALL_IW = HEAD_DIM + N_HEADS


def _cparams(*sem):
    return pltpu.CompilerParams(dimension_semantics=sem, vmem_limit_bytes=VMEM_LIMIT)


def _ln_rows(x, g, b):
    mu = jnp.mean(x, axis=-1, keepdims=True)
    xc = x - mu
    var = jnp.mean(xc * xc, axis=-1, keepdims=True)
    return xc * lax.rsqrt(var + LN_EPS) * g + b


def _sigmoid(x):
    return 1.0 / (1.0 + jnp.exp(-x))


def _dot(a, b):
    return jnp.dot(a, b, preferred_element_type=F32)


def _dot_nt(a, b):
    return lax.dot_general(a, b, (((1,), (1,)), ((), ())), preferred_element_type=F32)


def _ln_kernel(x_ref, g_ref, b_ref, o_ref):
    o_ref[...] = _ln_rows(x_ref[...], g_ref[...], b_ref[...])


def _layer_norm(x2d, g, b, tm):
    n, d = x2d.shape
    return pl.pallas_call(
        _ln_kernel,
        grid=(n // tm,),
        in_specs=[pl.BlockSpec((tm, d), lambda i: (i, 0)),
                  pl.BlockSpec((1, d), lambda i: (0, 0)),
                  pl.BlockSpec((1, d), lambda i: (0, 0))],
        out_specs=pl.BlockSpec((tm, d), lambda i: (i, 0)),
        out_shape=jax.ShapeDtypeStruct((n, d), F32),
        compiler_params=_cparams("parallel"),
        name="ln_in",
    )(x2d, g.reshape(1, d), b.reshape(1, d))


_PK = {"pool": 0, "fq": 256, "fk": 512, "fv": 768, "dq": 1024, "iq": 1280, "conv": 1536,
       "dkv": 2048, "small": 2176}
PK_COLS = 2304


def _pack_proj_weight(w):
    d = w.shape[0]
    sl = lambda k: w[:, _OFF[k][0]:_OFF[k][1]]
    small = jnp.concatenate([sl("ik"), sl("ff"), sl("iw"),
                             jnp.zeros((d, LANES - HEAD_DIM - 2 * N_HEADS), w.dtype)], axis=1)
    packed = jnp.concatenate([sl("pool"), sl("fq") * ATTN_SCALE, sl("fk"), sl("fv"),
                              sl("dq") * ATTN_SCALE, sl("iq"), sl("conv"), sl("dk"), sl("dv"),
                              small], axis=1)
    return packed.astype(BF16)


def _proj_kernel(h_ref, w_ref, pool_o, fq_o, fk_o, fv_o, dq_o, iq_o, glu_o, dk_o, dv_o, ik_o,
                 small_o):
    hb = h_ref[0].astype(BF16)

    def piece(name, width):
        return _dot(hb, w_ref[:, _PK[name]:_PK[name] + width])

    pool_o[0] = piece("pool", BRANCH_W)
    for name, out in (("fq", fq_o), ("fk", fk_o), ("fv", fv_o), ("dq", dq_o), ("iq", iq_o)):
        r = piece(name, BRANCH_W).astype(BF16)
        for h in range(N_HEADS):
            out[0, h] = r[:, h * HEAD_DIM:(h + 1) * HEAD_DIM]
    a = piece("conv", 2 * BRANCH_W)
    glu_o[0] = a[:, :BRANCH_W] * _sigmoid(a[:, BRANCH_W:])
    kv = piece("dkv", 2 * HEAD_DIM).astype(BF16)
    dk_o[0] = kv[:, :HEAD_DIM]
    dv_o[0] = kv[:, HEAD_DIM:]
    sm = piece("small", LANES)
    small_o[0] = sm
    ik_o[0] = sm[:, :HEAD_DIM].astype(BF16)


def _projection(h, wp, tm):
    b, s, d = h.shape
    heads = lambda: pl.BlockSpec((1, N_HEADS, tm, HEAD_DIM), lambda bi, i: (bi, 0, i, 0))
    rows = lambda w: pl.BlockSpec((1, tm, w), lambda bi, i: (bi, i, 0))
    head_shape = jax.ShapeDtypeStruct((b, N_HEADS, s, HEAD_DIM), BF16)
    return pl.pallas_call(
        _proj_kernel,
        grid=(b, s // tm),
        in_specs=[rows(d), pl.BlockSpec((d, PK_COLS), lambda bi, i: (0, 0))],
        out_specs=[rows(BRANCH_W), heads(), heads(), heads(), heads(), heads(), rows(BRANCH_W),
                   rows(HEAD_DIM), rows(HEAD_DIM), rows(HEAD_DIM), rows(LANES)],
        out_shape=[jax.ShapeDtypeStruct((b, s, BRANCH_W), F32), head_shape, head_shape, head_shape,
                   head_shape, head_shape, jax.ShapeDtypeStruct((b, s, BRANCH_W), F32),
                   jax.ShapeDtypeStruct((b, s, HEAD_DIM), BF16),
                   jax.ShapeDtypeStruct((b, s, HEAD_DIM), BF16),
                   jax.ShapeDtypeStruct((b, s, HEAD_DIM), BF16),
                   jax.ShapeDtypeStruct((b, s, LANES), F32)],
        compiler_params=_cparams("parallel", "parallel"),
        name="proj",
    )(h, wp)


def _forget_cumsum_kernel(f_ref, b_ref, o_ref):
    s = f_ref.shape[2]
    x = f_ref[0] + b_ref[...]
    ls = jnp.minimum(x, 0.0) - jnp.log(1.0 + jnp.exp(-jnp.abs(x)))
    r = lax.broadcasted_iota(I32, (LANES, LANES), 0)
    c = lax.broadcasted_iota(I32, (LANES, LANES), 1)
    upper = jnp.where(r <= c, 1.0, 0.0).astype(F32)
    carry = jnp.zeros((x.shape[0], 1), F32)
    for j in range(s // LANES):
        blk = jnp.dot(ls[:, j * LANES:(j + 1) * LANES], upper, preferred_element_type=F32,
                      precision=lax.Precision.HIGHEST) + carry
        carry = blk[:, LANES - 1:LANES]
        rest = blk
        for piece in range(3):
            part = rest.astype(BF16)
            o_ref[0, piece, :, j * LANES:(j + 1) * LANES] = part
            rest = rest - part.astype(F32)


def _forget_cumsum(f_t, b_col):
    b, r, s = f_t.shape
    return pl.pallas_call(
        _forget_cumsum_kernel,
        grid=(b,),
        in_specs=[pl.BlockSpec((1, r, s), lambda bi: (bi, 0, 0)),
                  pl.BlockSpec((r, 1), lambda bi: (0, 0))],
        out_specs=pl.BlockSpec((1, 3, r, s), lambda bi: (bi, 0, 0, 0)),
        out_shape=jax.ShapeDtypeStruct((b, 3, r, s), BF16),
        compiler_params=_cparams("parallel"),
        name="forget_cumsum",
    )(f_t, b_col)


def _local_kernel(pv_ref, pvh_ref, gl_ref, glh_ref, pw_ref, ps_ref, dw_ref, cb_ref, g_ref, b_ref,
                  a_o, d_o, pbuf, cbuf, *, ts):
    i = pl.program_id(1)
    first = i == 0
    pbuf[0:HALO, :] = jnp.where(first, 0.0, pvh_ref[0])
    pbuf[HALO:HALO + ts, :] = pv_ref[0]
    cbuf[0:HALO, :] = jnp.where(first, 0.0, glh_ref[0])
    cbuf[HALO:HALO + ts, :] = gl_ref[0]

    lane = lax.broadcasted_iota(I32, (ts, BRANCH_W), 1)
    pos = (i * ts + 1 + lax.broadcasted_iota(I32, (ts, 1), 0)).astype(F32)
    v = pbuf[HALO:HALO + ts, :]
    run = v
    pooled = jnp.zeros((ts, BRANCH_W), F32)
    shift = 1
    for g, w in enumerate(POOL_WINDOWS):
        while shift < w:
            run = run + pbuf[HALO - shift:HALO - shift + ts, :]
            shift += 1
        in_group = (lane >= g * POOL_GROUP) & (lane < (g + 1) * POOL_GROUP)
        pooled = jnp.where(in_group, run / jnp.minimum(pos, float(w)), pooled)
    pooled = pooled - v
    a_o[0] = (_dot(pooled.astype(BF16), pw_ref[...]) * ps_ref[...]).astype(BF16)

    y = jnp.zeros((ts, BRANCH_W), F32) + cb_ref[...]
    for j in range(CONV_W):
        back = CONV_W - 1 - j
        y = y + cbuf[HALO - back:HALO - back + ts, :] * dw_ref[j:j + 1, :]
    z = _ln_rows(y, g_ref[...], b_ref[...])
    d_o[0] = (z * _sigmoid(z)).astype(BF16)


def _local_mixers(pool_v, glu, pool_wbd, pool_scale, conv_dw, conv_b, ln_g, ln_b, ts):
    b, s, c = pool_v.shape
    cur = pl.BlockSpec((1, ts, c), lambda bi, i: (bi, i, 0))
    halo = pl.BlockSpec((1, HALO, c), lambda bi, i: (bi, jnp.maximum(i * (ts // HALO) - 1, 0), 0))
    full2 = lambda shp: pl.BlockSpec(shp, lambda bi, i: (0, 0))
    out_shape = jax.ShapeDtypeStruct((b, s, c), BF16)
    return pl.pallas_call(
        functools.partial(_local_kernel, ts=ts),
        grid=(b, s // ts),
        in_specs=[cur, halo, cur, halo, full2((c, c)), full2((1, c)), full2((CONV_W, c)),
                  full2((1, c)), full2((1, c)), full2((1, c))],
        out_specs=[cur, cur],
        out_shape=[out_shape, out_shape],
        scratch_shapes=[pltpu.VMEM((HALO + ts, c), F32), pltpu.VMEM((HALO + ts, c), F32)],
        compiler_params=_cparams("parallel", "parallel"),
        name="local_mixers",
    )(pool_v, pool_v, glu, glu, pool_wbd, pool_scale.reshape(1, c), conv_dw, conv_b.reshape(1, c),
      ln_g.reshape(1, c), ln_b.reshape(1, c))


AUG = 2 * HEAD_DIM


SUBLANES = 8
FOLD_CHAINS = 4


def _fold_rows(x, op):
    slabs = [x[i:i + SUBLANES] for i in range(0, x.shape[0], SUBLANES)]
    accs = slabs[:FOLD_CHAINS]
    for i, slab in enumerate(slabs[FOLD_CHAINS:]):
        accs[i % FOLD_CHAINS] = op(accs[i % FOLD_CHAINS], slab)
    while len(accs) > 1:
        accs = [op(accs[i], accs[i + 1]) for i in range(0, len(accs) - 1, 2)] + (
            [accs[-1]] if len(accs) % 2 else [])
    return accs[0]


def _softmax_chunk(s, m, l):
    m_new = jnp.maximum(m, jnp.max(_fold_rows(s, jnp.maximum), axis=0, keepdims=True))
    alpha = jnp.exp(m - m_new)
    p = jnp.exp(s - m_new)
    l_new = alpha * l + jnp.sum(_fold_rows(p, jnp.add), axis=0, keepdims=True)
    return m_new, l_new, alpha, p.astype(BF16)


def _causal_split(qi, tq, ck):
    return (qi * tq + 1) // ck, ((qi + 1) * tq + ck - 1) // ck


CHUNKS_PER_STEP = 2


def _attention_loops(n_plain, n_chunks, tq, scores, edit_last, values):
    def step(j0, carry, n, edit):
        ss = [scores(j0 + c) for c in range(n)]
        carry = list(carry)
        for c in range(n):
            stats, ps = [], []
            for h in range(N_HEADS):
                s = edit_last(j0 + c, h, ss[c][h]) if edit else ss[c][h]
                m_new, l_new, alpha, p = _softmax_chunk(s, carry[h][0], carry[h][1])
                stats.append((m_new, l_new, alpha))
                ps.append(p)
            for h in range(N_HEADS):
                m_new, l_new, alpha = stats[h]
                carry[h] = (m_new, l_new, alpha * carry[h][2] + _dot(values(j0 + c, h), ps[h]))
        return tuple(carry)

    carry = tuple((jnp.full((1, tq), NEG_INF, F32), jnp.zeros((1, tq), F32),
                   jnp.zeros((HEAD_DIM, tq), F32)) for _ in range(N_HEADS))
    n_multi = n_plain // CHUNKS_PER_STEP
    carry = lax.fori_loop(
        0, n_multi, lambda i, c: step(i * CHUNKS_PER_STEP, c, CHUNKS_PER_STEP, False), carry)
    carry = lax.fori_loop(n_multi * CHUNKS_PER_STEP, n_plain,
                          lambda j, c: step(j, c, 1, False), carry)
    if edit_last is None:
        return carry
    return lax.fori_loop(n_plain, n_chunks, lambda j, c: step(j, c, 1, True), carry)


def _fox_kernel(q_ref, k_ref, vt_ref, o_ref, *, tq, ck):
    qi = pl.program_id(1)
    n_full, n_chunks = _causal_split(qi, tq, ck)
    qpos = qi * tq + lax.broadcasted_iota(I32, (1, tq), 1)
    kiota = lax.broadcasted_iota(I32, (ck, 1), 0)
    qs = [q_ref[0, h] for h in range(N_HEADS)]

    def scores(j):
        start = pl.multiple_of(j * ck, ck)
        return [_dot_nt(k_ref[0, h, pl.ds(start, ck), :], qs[h]) for h in range(N_HEADS)]

    def causal(j, h, s):
        return jnp.where(j * ck + kiota <= qpos, s, NEG_INF)

    def values(j, h):
        return vt_ref[0, h, j]

    carry = _attention_loops(n_full, n_chunks, tq, scores, causal, values)
    for h in range(N_HEADS):
        _, l, acc = carry[h]
        o_ref[0, h * HEAD_DIM:(h + 1) * HEAD_DIM, :] = (acc / l).astype(BF16)


def _fox_attention(fq, fk, fv, c_parts, tq, ck):
    b, nh, s, dh = fq.shape
    n_ck = s // ck
    pad = jnp.zeros((b, nh, s, AUG - dh - 3), BF16)
    q_aug = jnp.concatenate([fq, jnp.full((b, nh, s, 3), -1.0, BF16), pad], axis=-1)
    k_aug = jnp.concatenate([fk, jnp.moveaxis(c_parts, 1, 3), pad], axis=-1)
    v_t = jnp.swapaxes(fv.reshape(b, nh, n_ck, ck, dh), 3, 4)
    return pl.pallas_call(
        functools.partial(_fox_kernel, tq=tq, ck=ck),
        grid=(b, s // tq),
        in_specs=[pl.BlockSpec((1, nh, tq, AUG), lambda bi, i: (bi, 0, i, 0)),
                  pl.BlockSpec((1, nh, s, AUG), lambda bi, i: (bi, 0, 0, 0)),
                  pl.BlockSpec((1, nh, n_ck, dh, ck), lambda bi, i: (bi, 0, 0, 0, 0))],
        out_specs=pl.BlockSpec((1, nh * dh, tq), lambda bi, i: (bi, 0, i)),
        out_shape=jax.ShapeDtypeStruct((b, nh * dh, s), BF16),
        compiler_params=_cparams("parallel", "arbitrary"),
        name="fox_attention",
    )(q_aug, k_aug, v_t)


def _key_to_f32(key):
    bits = key ^ (lax.shift_right_arithmetic(key, 31) & 0x7FFFFFFF)
    return lax.bitcast_convert_type(bits, F32)


def _f32_to_key(x):
    bits = lax.bitcast_convert_type(x, I32)
    return bits ^ (lax.shift_right_arithmetic(bits, 31) & 0x7FFFFFFF)


KEY_POS_INF = 0x7F800000
KEY_NEG_INF = 0x007FFFFF - 2 ** 31
PROBE_STEP = 1 << 24
MAX_PROBES = 36


def _dsa_kernel(dq_ref, iq_ref, iwt_ref, ik_ref, dk_ref, dvt_ref, o_ref, score_buf, bias_buf, *,
                tq, ck, topk):
    qi = pl.program_id(1)
    n_full, n_chunks = _causal_split(qi, tq, ck)
    qpos = qi * tq + lax.broadcasted_iota(I32, (1, tq), 1)
    kiota = lax.broadcasted_iota(I32, (ck, 1), 0)
    w_idx = iwt_ref[0]
    kf = float(topk)

    def count(pred):
        def cb(j, acc):
            return acc + _fold_rows(jnp.where(pred(score_buf[j]), 1.0, 0.0), jnp.add)
        acc = lax.fori_loop(0, n_chunks, cb, jnp.zeros((SUBLANES, tq), F32))
        return jnp.sum(acc, axis=0, keepdims=True)

    def score_body(j, _, masked):
        start = pl.multiple_of(j * ck, ck)
        ikc = ik_ref[0, pl.ds(start, ck), :]
        sc = None
        for h in range(N_HEADS):
            term = jnp.maximum(_dot_nt(ikc, iq_ref[0, h]), 0.0) * w_idx[h:h + 1, :]
            sc = term if sc is None else sc + term
        if masked:
            sc = jnp.where(start + kiota <= qpos, sc, NEG_INF)
        score_buf[j] = sc
        return 0

    lax.fori_loop(0, n_full, functools.partial(score_body, masked=False), 0)
    lax.fori_loop(n_full, n_chunks, functools.partial(score_body, masked=True), 0)

    def row_max():
        def cb(j, acc):
            return jnp.maximum(acc, _fold_rows(score_buf[j], jnp.maximum))
        acc = lax.fori_loop(0, n_chunks, cb, jnp.full((SUBLANES, tq), -jnp.inf, F32))
        return jnp.max(acc, axis=0, keepdims=True)

    top_key = _f32_to_key(row_max())
    c_pos = count(lambda s: s > 0.0)
    c_nn = count(lambda s: s >= 0.0)
    pos_side, zero_tie = c_pos >= kf, c_nn >= kf
    n_tot = (n_chunks * ck).astype(F32)
    lo = jnp.where(pos_side, 1, jnp.where(zero_tie, 0, KEY_NEG_INF)).astype(I32)
    hi = jnp.where(pos_side, KEY_POS_INF, jnp.where(zero_tie, 1, 0)).astype(I32)
    c_lo = jnp.where(pos_side, c_pos, jnp.where(zero_tie, c_nn, n_tot))

    def unsettled(lo, hi, c_lo):
        return (c_lo != kf) & (hi > lo + 1)

    def probe(state):
        it, lo, hi, c_lo = state
        mid = (lo >> 1) + (hi >> 1) + (lo & hi & 1)
        near_top = jnp.where(hi > 0, jnp.maximum(mid, hi - PROBE_STEP), mid)
        cand = jnp.where(it == 0, top_key, jnp.where(it == 1, near_top, mid))
        cand = jnp.minimum(jnp.maximum(cand, lo + 1), hi - 1)
        cand_f = _key_to_f32(cand)
        c = count(lambda s: s >= cand_f)
        live = unsettled(lo, hi, c_lo)
        up = live & (c >= kf)
        down = live & (c < kf)
        return (it + 1, jnp.where(up, cand, lo), jnp.where(down, cand, hi),
                jnp.where(up, c, c_lo))

    def searching(state):
        it, lo, hi, c_lo = state
        left = jnp.max(jnp.where(unsettled(lo, hi, c_lo), 1.0, 0.0))
        return (it < MAX_PROBES) & (left > 0.0)

    _, lo, _, _ = lax.while_loop(searching, lambda st: probe(probe(st)),
                                 (jnp.int32(0), lo, hi, c_lo))
    thr = _key_to_f32(lo)

    need = kf - count(lambda s: s > thr)
    r_i = lax.broadcasted_iota(I32, (ck, ck), 0)
    c_i = lax.broadcasted_iota(I32, (ck, ck), 1)
    lower = jnp.where(c_i <= r_i, 1.0, 0.0).astype(BF16)

    def bias_body(j, seen, masked):
        start = pl.multiple_of(j * ck, ck)
        sc = score_buf[j]
        tie = sc == thr
        rank = _dot(lower, jnp.where(tie, 1.0, 0.0).astype(BF16)) + seen
        keep = jnp.where(sc > thr, 0.0, jnp.where(tie, jnp.where(rank <= need, 0.0, NEG_INF),
                                                  NEG_INF))
        if masked:
            keep = jnp.where(start + kiota <= qpos, keep, NEG_INF)
        bias_buf[j] = keep
        return rank[ck - 1:ck, :]

    seen = lax.fori_loop(0, n_full, functools.partial(bias_body, masked=False),
                         jnp.zeros((1, tq), F32))
    lax.fori_loop(n_full, n_chunks, functools.partial(bias_body, masked=True), seen)

    qs = [dq_ref[0, h] for h in range(N_HEADS)]

    def scores(j):
        start = pl.multiple_of(j * ck, ck)
        kc = dk_ref[0, pl.ds(start, ck), :]
        bias = bias_buf[j]
        return [_dot_nt(kc, qs[h]) + bias for h in range(N_HEADS)]

    carry = _attention_loops(n_chunks, n_chunks, tq, scores, None, lambda j, h: dvt_ref[0, j])
    for h in range(N_HEADS):
        _, l, acc = carry[h]
        o_ref[0, h * HEAD_DIM:(h + 1) * HEAD_DIM, :] = (acc / l).astype(BF16)


def _alibi_columns(s):
    pos = jnp.arange(s, dtype=I32)
    hi, lo = (pos // 64).astype(F32), (pos % 64).astype(F32)
    one = jnp.ones((s,), F32)
    k_cols = jnp.stack([hi, lo, one, one], axis=-1)
    q_cols = []
    for h in range(N_HEADS):
        slope = 2.0 ** (-8.0 * (h + 1) / N_HEADS)
        q_cols.append(jnp.stack([64.0 * slope * one, slope * one, -64.0 * slope * hi, -slope * lo],
                                axis=-1))
    return k_cols.astype(BF16), jnp.stack(q_cols, axis=0).astype(BF16)


def _dsa_attention(dq, iq, iw_t, ik, dk, dv, tq, ck):
    b, nh, s, dh = dq.shape
    assert s <= 64 * 64, "position columns hold two base-64 digits"
    topk = min(DSA_TOPK, s // 4)
    assert ck >= topk, "every query must see at least topk (possibly masked) candidates"
    n_ck = s // ck
    k_cols, q_cols = _alibi_columns(s)
    q_aug = jnp.concatenate([dq, jnp.broadcast_to(q_cols[None], (b, nh, s, 4)),
                             jnp.zeros((b, nh, s, AUG - dh - 4), BF16)], axis=-1)
    k_aug = jnp.concatenate([dk, jnp.broadcast_to(k_cols[None], (b, s, 4)),
                             jnp.zeros((b, s, AUG - dh - 4), BF16)], axis=-1)
    v_t = jnp.swapaxes(dv.reshape(b, n_ck, ck, dh), 2, 3)
    return pl.pallas_call(
        functools.partial(_dsa_kernel, tq=tq, ck=ck, topk=topk),
        grid=(b, s // tq),
        in_specs=[pl.BlockSpec((1, nh, tq, AUG), lambda bi, i: (bi, 0, i, 0)),
                  pl.BlockSpec((1, nh, tq, dh), lambda bi, i: (bi, 0, i, 0)),
                  pl.BlockSpec((1, 8, tq), lambda bi, i: (bi, 0, i)),
                  pl.BlockSpec((1, s, dh), lambda bi, i: (bi, 0, 0)),
                  pl.BlockSpec((1, s, AUG), lambda bi, i: (bi, 0, 0)),
                  pl.BlockSpec((1, n_ck, dh, ck), lambda bi, i: (bi, 0, 0, 0))],
        out_specs=pl.BlockSpec((1, nh * dh, tq), lambda bi, i: (bi, 0, i)),
        out_shape=jax.ShapeDtypeStruct((b, nh * dh, s), BF16),
        scratch_shapes=[pltpu.VMEM((n_ck, ck, tq), F32), pltpu.VMEM((n_ck, ck, tq), F32)],
        compiler_params=_cparams("parallel", "arbitrary"),
        name="dsa_attention",
    )(q_aug, iq, iw_t, ik, k_aug, v_t)


def _merge_kernel(h_ref, a_ref, b_ref, c_ref, d_ref, wg_ref, bg_ref, wb_ref, wo_ref, g_ref,
                  beta_ref, o_ref, *, alpha):
    h = h_ref[...]
    hb = h.astype(BF16)
    d = h.shape[1]
    acc = jnp.zeros(h.shape, F32)
    for n, br in enumerate((a_ref, b_ref, c_ref, d_ref)):
        gate = _sigmoid(_dot(hb, wg_ref[:, n * d:(n + 1) * d]) + bg_ref[:, n * d:(n + 1) * d])
        acc = acc + gate * _dot(br[...], wb_ref[n])
    mixed = _dot(acc.astype(BF16), wo_ref[...])
    o_ref[...] = _ln_rows(alpha * h + mixed, g_ref[...], beta_ref[...])


def _merge(h2d, branches, w_gate, b_gate, w_branch, w_out, ln_g, ln_b, alpha, tm):
    n, d = h2d.shape
    c = branches[0].shape[1]
    rows = lambda w: pl.BlockSpec((tm, w), lambda i: (i, 0))
    const = lambda shp: pl.BlockSpec(shp, lambda i: (0,) * len(shp))
    return pl.pallas_call(
        functools.partial(_merge_kernel, alpha=alpha),
        grid=(n // tm,),
        in_specs=[rows(d), rows(c), rows(c), rows(c), rows(c), const((d, N_BRANCH * d)),
                  const((1, N_BRANCH * d)), const((N_BRANCH, c, d)), const((d, d)),
                  const((1, d)), const((1, d))],
        out_specs=rows(d),
        out_shape=jax.ShapeDtypeStruct((n, d), F32),
        compiler_params=_cparams("parallel"),
        name="merge",
    )(h2d, *branches, w_gate, b_gate.reshape(1, -1), w_branch, w_out, ln_g.reshape(1, d),
      ln_b.reshape(1, d))


ROUTER_E0 = N_GROUPS
MOE_BLK = 256


def _routing_t(logits):
    row = lax.broadcasted_iota(I32, logits.shape, 0)
    far = jnp.int32(4 * LANES)
    neg = -jnp.inf

    def first_max(x):
        mx = jnp.max(x, axis=0, keepdims=True)
        idx = jnp.min(jnp.where(x == mx, row, far), axis=0, keepdims=True)
        return mx, idx

    lg = jnp.where(row < N_GROUPS, logits, neg)
    g_max, g_idx = first_max(lg)
    p_group = 1.0 / jnp.sum(jnp.exp(lg - g_max), axis=0, keepdims=True)
    lo = ROUTER_E0 + EXPERTS_PER_GROUP * g_idx
    le = jnp.where((row >= lo) & (row < lo + EXPERTS_PER_GROUP), logits, neg)
    m1, e1 = first_max(le)
    m2, e2 = first_max(jnp.where(row == e1, neg, le))
    r = jnp.exp(m2 - m1)
    gate1 = p_group * (1.0 / (1.0 + r))
    gate2 = p_group * (r / (1.0 + r))
    return jnp.where(row == e1, gate1, jnp.where(row == e2, gate2, 0.0)), g_idx


def _bf16_pieces(x, n):
    out, rest = [], x
    for _ in range(n):
        part = rest.astype(BF16)
        out.append(part)
        rest = rest - part.astype(F32)
    return out


def _moe_kernel(h_ref, wr_ref, br_ref, wgu_ref, wd_ref, g_ref, beta_ref, o_ref, blk_ref, p_ref,
                xs_ref, gs_ref, acc_ref, *, alpha):
    e = pl.program_id(1)
    t, d = h_ref.shape
    f = wd_ref.shape[2]
    n_slots = p_ref.shape[0] // MOE_BLK

    @pl.when(e == 0)
    def _():
        h = h_ref[...]
        hb = h.astype(BF16)
        gates, g_idx = _routing_t(_dot_nt(wr_ref[...], hb) + br_ref[...])
        grp = lax.broadcasted_iota(I32, (SUBLANES, t), 0)
        onehot = jnp.where(grp == g_idx, 1.0, 0.0)
        r_i = lax.broadcasted_iota(I32, (t, t), 0)
        c_i = lax.broadcasted_iota(I32, (t, t), 1)
        before = jnp.where(r_i < c_i, 1.0, 0.0).astype(BF16)
        rank = _dot(onehot.astype(BF16), before)
        total = jnp.sum(onehot, axis=1, keepdims=True)
        start = jnp.int32(0)
        starts = jnp.zeros((SUBLANES, 1), F32)
        grp_col = lax.broadcasted_iota(I32, (SUBLANES, 1), 0)
        for g in range(N_GROUPS):
            n_g = (total[g, 0].astype(I32) + (MOE_BLK - 1)) // MOE_BLK
            blk_ref[2 * g] = start
            blk_ref[2 * g + 1] = n_g
            starts = jnp.where(grp_col == g, (start * MOE_BLK).astype(F32), starts)
            start = start + n_g
        blk_ref[2 * N_GROUPS] = start
        pos = jnp.sum(onehot * (starts + rank), axis=0, keepdims=True).astype(I32)
        slot = lax.broadcasted_iota(I32, (p_ref.shape[0], t), 0)
        p_ref[...] = jnp.where(slot == pos, 1.0, 0.0).astype(BF16)
        perm = p_ref[...]
        xs = _dot_nt(h.T.astype(BF16), perm).astype(BF16)
        gs = sum(_dot_nt(piece, perm) for piece in _bf16_pieces(gates, 3))
        for b in range(n_slots):
            cols = slice(b * MOE_BLK, (b + 1) * MOE_BLK)
            xs_ref[b] = xs[:, cols]
            acc_ref[b] = jnp.zeros((d, MOE_BLK), F32)
            for k in range(N_EXPERTS):
                gs_ref[b, k] = gs[ROUTER_E0 + k:ROUTER_E0 + k + 1, cols]

    grp_e = e // EXPERTS_PER_GROUP
    blk0 = blk_ref[2 * grp_e]

    def expert_block(i, _):
        b = blk0 + i
        xs = xs_ref[b]
        hg = _dot(wgu_ref[0, :f], xs)
        hu = _dot(wgu_ref[0, f:], xs)
        hid = (hg * _sigmoid(hg) * hu).astype(BF16)
        y = jnp.concatenate([_dot(wd_ref[0, :d // 2], hid), _dot(wd_ref[0, d // 2:], hid)], axis=0)
        acc_ref[b] += gs_ref[b, e] * y
        return 0

    lax.fori_loop(0, blk_ref[2 * grp_e + 1], expert_block, 0)

    @pl.when(e == pl.num_programs(1) - 1)
    def _():
        pieces = [jnp.concatenate(col, axis=1) for col in
                  zip(*[_bf16_pieces(acc_ref[b], 2) for b in range(n_slots)])]
        half = t // 2
        for c in range(2):
            cols = slice(c * half, (c + 1) * half)
            out_t = sum(_dot(piece, p_ref[:, cols]) for piece in pieces)
            o_ref[cols, :] = _ln_rows(alpha * h_ref[cols, :] + out_t.T, g_ref[...], beta_ref[...])


def _moe(h2d, w_router_t, b_router_t, wgu_t, wd_t, ln_g, ln_b, alpha, tm):
    n, d = h2d.shape
    ne, _, f = wd_t.shape
    assert tm % MOE_BLK == 0
    n_slots = tm // MOE_BLK + N_GROUPS
    rows = pl.BlockSpec((tm, d), lambda i, e: (i, 0))
    const = lambda shp: pl.BlockSpec(shp, lambda i, e: (0,) * len(shp))
    return pl.pallas_call(
        functools.partial(_moe_kernel, alpha=alpha),
        grid=(n // tm, ne),
        in_specs=[rows, const((LANES, d)), const((LANES, 1)),
                  pl.BlockSpec((1, 2 * f, d), lambda i, e: (e, 0, 0)),
                  pl.BlockSpec((1, d, f), lambda i, e: (e, 0, 0)),
                  const((1, d)), const((1, d))],
        out_specs=rows,
        out_shape=jax.ShapeDtypeStruct((n, d), F32),
        scratch_shapes=[pltpu.SMEM((2 * N_GROUPS + 1,), I32),
                        pltpu.VMEM((n_slots * MOE_BLK, tm), BF16),
                        pltpu.VMEM((n_slots, d, MOE_BLK), BF16),
                        pltpu.VMEM((n_slots, ne, 1, MOE_BLK), F32),
                        pltpu.VMEM((n_slots, d, MOE_BLK), F32)],
        compiler_params=pltpu.CompilerParams(dimension_semantics=("parallel", "arbitrary"),
                                             vmem_limit_bytes=MOE_VMEM_LIMIT),
        name="moe",
    )(h2d, w_router_t, b_router_t, wgu_t, wd_t, ln_g.reshape(1, d), ln_b.reshape(1, d))


def _tiles(b, s):
    n = b * s
    pick = lambda total, want: want if total % want == 0 else total
    return dict(
        ln_tm=pick(n, 512), proj_tm=pick(s, 512), local_ts=pick(s, 512),
        fox_tq=pick(s, 256), fox_ck=pick(s, 512), dsa_tq=pick(s, 256), dsa_ck=pick(s, 512),
        merge_tm=pick(n, 512), moe_tm=pick(n, 1024))


def kernel(x, ln_in_g, ln_in_b, w_in, b_forget, b_gate, pool_w, pool_scale, conv_dw, conv_b,
           conv_ln_g, conv_ln_b, w_branch, w_out, ln1_g, ln1_b, router_g, router_g_b, router_e,
           router_e_b, expert_w_gate, expert_w_up, expert_w_down, ln2_g, ln2_b):
    b, s, d = x.shape
    depth = w_in.shape[0]
    alpha = (2.0 * depth) ** 0.25
    t = _tiles(b, s)
    n = b * s

    h = _layer_norm(x.reshape(n, d), ln_in_g, ln_in_b, t["ln_tm"])
    for l in range(depth):
        wp = _pack_proj_weight(w_in[l])
        (pool_v, fq, fk, fv, dq, iq, glu, dk, dv, ik, small) = _projection(
            h.reshape(b, s, d), wp, t["proj_tm"])

        small_t = jnp.swapaxes(small[..., SMALL_FF:SMALL_FF + 2 * N_HEADS], 1, 2)
        f_t = jnp.pad(small_t[:, :N_HEADS], ((0, 0), (0, 8 - N_HEADS), (0, 0)))
        iw_t = jnp.pad(small_t[:, N_HEADS:], ((0, 0), (0, 8 - N_HEADS), (0, 0)))
        bf_col = jnp.pad(b_forget[l], (0, 8 - N_HEADS)).reshape(8, 1)
        c_parts = _forget_cumsum(f_t, bf_col)[:, :, :N_HEADS]

        pool_wbd = jnp.zeros((BRANCH_W, BRANCH_W), F32)
        for g in range(len(POOL_WINDOWS)):
            sl = slice(g * POOL_GROUP, (g + 1) * POOL_GROUP)
            pool_wbd = pool_wbd.at[sl, sl].set(pool_w[l, g])
        br_a, br_d = _local_mixers(pool_v, glu, pool_wbd.astype(BF16), pool_scale[l], conv_dw[l],
                                   conv_b[l], conv_ln_g[l], conv_ln_b[l], t["local_ts"])
        br_b = _fox_attention(fq, fk, fv, c_parts, t["fox_tq"], t["fox_ck"])
        br_c = _dsa_attention(dq, iq, iw_t, ik, dk, dv, t["dsa_tq"], t["dsa_ck"])
        br_b, br_c = jnp.swapaxes(br_b, 1, 2), jnp.swapaxes(br_c, 1, 2)

        branches = [br.reshape(n, BRANCH_W) for br in (br_a, br_b, br_c, br_d)]
        h = _merge(h, branches, w_in[l][:, GATE_OFF:].astype(BF16), b_gate[l],
                   w_branch[l].astype(BF16), w_out[l].astype(BF16), ln1_g[l], ln1_b[l], alpha,
                   t["merge_tm"])

        w_router_t = jnp.concatenate(
            [router_g[l], router_e[l], jnp.zeros((d, LANES - N_GROUPS - N_EXPERTS), F32)],
            axis=1).T.astype(BF16)
        b_router_t = jnp.concatenate(
            [router_g_b[l], router_e_b[l], jnp.zeros((LANES - N_GROUPS - N_EXPERTS,), F32)]
        ).reshape(LANES, 1)
        wgu_t = jnp.concatenate([jnp.swapaxes(expert_w_gate[l], 1, 2),
                                 jnp.swapaxes(expert_w_up[l], 1, 2)], axis=1).astype(BF16)
        wd_t = jnp.swapaxes(expert_w_down[l], 1, 2).astype(BF16)
        h = _moe(h, w_router_t, b_router_t, wgu_t, wd_t, ln2_g[l], ln2_b[l], alpha, t["moe_tm"])
    return h.reshape(b, s, d)
```

```python
import functools

import jax
import jax.numpy as jnp
from jax import lax
from jax.experimental import pallas as pl
from jax.experimental.pallas import tpu as pltpu

F32 = jnp.float32
BF16 = jnp.bfloat16
I32 = jnp.int32

N_BRANCH = 4
HEAD_DIM = 64
N_HEADS = 4
BRANCH_W = N_HEADS * HEAD_DIM
POOL_WINDOWS = (2, 4, 8, 16)
POOL_GROUP = BRANCH_W // len(POOL_WINDOWS)
CONV_W = 31
DSA_TOPK = 256
N_GROUPS = 4
EXPERTS_PER_GROUP = 4
N_EXPERTS = N_GROUPS * EXPERTS_PER_GROUP
LN_EPS = 1e-5
NEG_INF = -1e30
ATTN_SCALE = HEAD_DIM ** -0.5

LANES = 128
HALO = 32
VMEM_LIMIT = 56 * 1024 * 1024
MOE_VMEM_LIMIT = 60 * 1024 * 1024

_OFF = {}
_o = 0
for _name, _n in (("pool", BRANCH_W), ("fq", BRANCH_W), ("fk", BRANCH_W), ("fv", BRANCH_W),
                  ("ff", N_HEADS), ("dq", BRANCH_W), ("dk", HEAD_DIM), ("dv", HEAD_DIM),
                  ("iq", BRANCH_W), ("ik", HEAD_DIM), ("iw", N_HEADS), ("conv", 2 * BRANCH_W)):
    _OFF[_name] = (_o, _o + _n)
    _o += _n
GATE_OFF = _o
SMALL_FF = HEAD_DIM
SMALL_IW = HEAD_DIM + N_HEADS


def _cparams(*sem):
    return pltpu.CompilerParams(dimension_semantics=sem, vmem_limit_bytes=VMEM_LIMIT)


def _ln_rows(x, g, b):
    mu = jnp.mean(x, axis=-1, keepdims=True)
    xc = x - mu
    var = jnp.mean(xc * xc, axis=-1, keepdims=True)
    return xc * lax.rsqrt(var + LN_EPS) * g + b


def _sigmoid(x):
    return 1.0 / (1.0 + jnp.exp(-x))


def _dot(a, b):
    return jnp.dot(a, b, preferred_element_type=F32)


def _dot_nt(a, b):
    return lax.dot_general(a, b, (((1,), (1,)), ((), ())), preferred_element_type=F32)


def _ln_kernel(x_ref, g_ref, b_ref, o_ref):
    o_ref[...] = _ln_rows(x_ref[...], g_ref[...], b_ref[...])


def _layer_norm(x2d, g, b, tm):
    n, d = x2d.shape
    return pl.pallas_call(
        _ln_kernel,
        grid=(n // tm,),
        in_specs=[pl.BlockSpec((tm, d), lambda i: (i, 0)),
                  pl.BlockSpec((1, d), lambda i: (0, 0)),
                  pl.BlockSpec((1, d), lambda i: (0, 0))],
        out_specs=pl.BlockSpec((tm, d), lambda i: (i, 0)),
        out_shape=jax.ShapeDtypeStruct((n, d), F32),
        compiler_params=_cparams("parallel"),
        name="ln_in",
    )(x2d, g.reshape(1, d), b.reshape(1, d))


_PK = {"pool": 0, "fq": 256, "fk": 512, "fv": 768, "dq": 1024, "iq": 1280, "conv": 1536,
       "dkv": 2048, "small": 2176}
PK_COLS = 2304


def _pack_proj_weight(w):
    d = w.shape[0]
    sl = lambda k: w[:, _OFF[k][0]:_OFF[k][1]]
    small = jnp.concatenate([sl("ik"), sl("ff"), sl("iw"),
                             jnp.zeros((d, LANES - HEAD_DIM - 2 * N_HEADS), w.dtype)], axis=1)
    packed = jnp.concatenate([sl("pool"), sl("fq") * ATTN_SCALE, sl("fk"), sl("fv"),
                              sl("dq") * ATTN_SCALE, sl("iq"), sl("conv"), sl("dk"), sl("dv"),
                              small], axis=1)
    return packed.astype(BF16)


def _proj_kernel(h_ref, w_ref, pool_o, fq_o, fk_o, fv_o, dq_o, iq_o, glu_o, dk_o, dv_o, ik_o,
                 small_o):
    hb = h_ref[0].astype(BF16)

    def piece(name, width):
        return _dot(hb, w_ref[:, _PK[name]:_PK[name] + width])

    pool_o[0] = piece("pool", BRANCH_W)
    for name, out in (("fq", fq_o), ("fk", fk_o), ("fv", fv_o), ("dq", dq_o), ("iq", iq_o)):
        r = piece(name, BRANCH_W).astype(BF16)
        for h in range(N_HEADS):
            out[0, h] = r[:, h * HEAD_DIM:(h + 1) * HEAD_DIM]
    a = piece("conv", 2 * BRANCH_W)
    glu_o[0] = a[:, :BRANCH_W] * _sigmoid(a[:, BRANCH_W:])
    kv = piece("dkv", 2 * HEAD_DIM).astype(BF16)
    dk_o[0] = kv[:, :HEAD_DIM]
    dv_o[0] = kv[:, HEAD_DIM:]
    sm = piece("small", LANES)
    small_o[0] = sm
    ik_o[0] = sm[:, :HEAD_DIM].astype(BF16)


def _projection(h, wp, tm):
    b, s, d = h.shape
    heads = lambda: pl.BlockSpec((1, N_HEADS, tm, HEAD_DIM), lambda bi, i: (bi, 0, i, 0))
    rows = lambda w: pl.BlockSpec((1, tm, w), lambda bi, i: (bi, i, 0))
    head_shape = jax.ShapeDtypeStruct((b, N_HEADS, s, HEAD_DIM), BF16)
    return pl.pallas_call(
        _proj_kernel,
        grid=(b, s // tm),
        in_specs=[rows(d), pl.BlockSpec((d, PK_COLS), lambda bi, i: (0, 0))],
        out_specs=[rows(BRANCH_W), heads(), heads(), heads(), heads(), heads(), rows(BRANCH_W),
                   rows(HEAD_DIM), rows(HEAD_DIM), rows(HEAD_DIM), rows(LANES)],
        out_shape=[jax.ShapeDtypeStruct((b, s, BRANCH_W), F32), head_shape, head_shape, head_shape,
                   head_shape, head_shape, jax.ShapeDtypeStruct((b, s, BRANCH_W), F32),
                   jax.ShapeDtypeStruct((b, s, HEAD_DIM), BF16),
                   jax.ShapeDtypeStruct((b, s, HEAD_DIM), BF16),
                   jax.ShapeDtypeStruct((b, s, HEAD_DIM), BF16),
                   jax.ShapeDtypeStruct((b, s, LANES), F32)],
        compiler_params=_cparams("parallel", "parallel"),
        name="proj",
    )(h, wp)


def _forget_cumsum_kernel(f_ref, b_ref, o_ref):
    s = f_ref.shape[2]
    x = f_ref[0] + b_ref[...]
    ls = jnp.minimum(x, 0.0) - jnp.log(1.0 + jnp.exp(-jnp.abs(x)))
    r = lax.broadcasted_iota(I32, (LANES, LANES), 0)
    c = lax.broadcasted_iota(I32, (LANES, LANES), 1)
    upper = jnp.where(r <= c, 1.0, 0.0).astype(F32)
    carry = jnp.zeros((x.shape[0], 1), F32)
    for j in range(s // LANES):
        blk = jnp.dot(ls[:, j * LANES:(j + 1) * LANES], upper, preferred_element_type=F32,
                      precision=lax.Precision.HIGHEST) + carry
        carry = blk[:, LANES - 1:LANES]
        rest = blk
        for piece in range(3):
            part = rest.astype(BF16)
            o_ref[0, piece, :, j * LANES:(j + 1) * LANES] = part
            rest = rest - part.astype(F32)


def _forget_cumsum(f_t, b_col):
    b, r, s = f_t.shape
    return pl.pallas_call(
        _forget_cumsum_kernel,
        grid=(b,),
        in_specs=[pl.BlockSpec((1, r, s), lambda bi: (bi, 0, 0)),
                  pl.BlockSpec((r, 1), lambda bi: (0, 0))],
        out_specs=pl.BlockSpec((1, 3, r, s), lambda bi: (bi, 0, 0, 0)),
        out_shape=jax.ShapeDtypeStruct((b, 3, r, s), BF16),
        compiler_params=_cparams("parallel"),
        name="forget_cumsum",
    )(f_t, b_col)


def _local_kernel(pv_ref, pvh_ref, gl_ref, glh_ref, pw_ref, ps_ref, dw_ref, cb_ref, g_ref, b_ref,
                  a_o, d_o, pbuf, cbuf, *, ts):
    i = pl.program_id(1)
    first = i == 0
    pbuf[0:HALO, :] = jnp.where(first, 0.0, pvh_ref[0])
    pbuf[HALO:HALO + ts, :] = pv_ref[0]
    cbuf[0:HALO, :] = jnp.where(first, 0.0, glh_ref[0])
    cbuf[HALO:HALO + ts, :] = gl_ref[0]

    lane = lax.broadcasted_iota(I32, (ts, BRANCH_W), 1)
    pos = (i * ts + 1 + lax.broadcasted_iota(I32, (ts, 1), 0)).astype(F32)
    v = pbuf[HALO:HALO + ts, :]
    run = v
    pooled = jnp.zeros((ts, BRANCH_W), F32)
    shift = 1
    for g, w in enumerate(POOL_WINDOWS):
        while shift < w:
            run = run + pbuf[HALO - shift:HALO - shift + ts, :]
            shift += 1
        in_group = (lane >= g * POOL_GROUP) & (lane < (g + 1) * POOL_GROUP)
        pooled = jnp.where(in_group, run / jnp.minimum(pos, float(w)), pooled)
    pooled = pooled - v
    a_o[0] = (_dot(pooled.astype(BF16), pw_ref[...]) * ps_ref[...]).astype(BF16)

    y = jnp.zeros((ts, BRANCH_W), F32) + cb_ref[...]
    for j in range(CONV_W):
        back = CONV_W - 1 - j
        y = y + cbuf[HALO - back:HALO - back + ts, :] * dw_ref[j:j + 1, :]
    z = _ln_rows(y, g_ref[...], b_ref[...])
    d_o[0] = (z * _sigmoid(z)).astype(BF16)


def _local_mixers(pool_v, glu, pool_wbd, pool_scale, conv_dw, conv_b, ln_g, ln_b, ts):
    b, s, c = pool_v.shape
    cur = pl.BlockSpec((1, ts, c), lambda bi, i: (bi, i, 0))
    halo = pl.BlockSpec((1, HALO, c), lambda bi, i: (bi, jnp.maximum(i * (ts // HALO) - 1, 0), 0))
    full2 = lambda shp: pl.BlockSpec(shp, lambda bi, i: (0, 0))
    out_shape = jax.ShapeDtypeStruct((b, s, c), BF16)
    return pl.pallas_call(
        functools.partial(_local_kernel, ts=ts),
        grid=(b, s // ts),
        in_specs=[cur, halo, cur, halo, full2((c, c)), full2((1, c)), full2((CONV_W, c)),
                  full2((1, c)), full2((1, c)), full2((1, c))],
        out_specs=[cur, cur],
        out_shape=[out_shape, out_shape],
        scratch_shapes=[pltpu.VMEM((HALO + ts, c), F32), pltpu.VMEM((HALO + ts, c), F32)],
        compiler_params=_cparams("parallel", "parallel"),
        name="local_mixers",
    )(pool_v, pool_v, glu, glu, pool_wbd, pool_scale.reshape(1, c), conv_dw, conv_b.reshape(1, c),
      ln_g.reshape(1, c), ln_b.reshape(1, c))


AUG = 2 * HEAD_DIM


SUBLANES = 8
FOLD_CHAINS = 8


def _fold_rows(x, op):
    slabs = [x[i:i + SUBLANES] for i in range(0, x.shape[0], SUBLANES)]
    accs = slabs[:FOLD_CHAINS]
    for i, slab in enumerate(slabs[FOLD_CHAINS:]):
        accs[i % FOLD_CHAINS] = op(accs[i % FOLD_CHAINS], slab)
    while len(accs) > 1:
        accs = [op(accs[i], accs[i + 1]) for i in range(0, len(accs) - 1, 2)] + (
            [accs[-1]] if len(accs) % 2 else [])
    return accs[0]


def _softmax_chunk(s, m, l):
    m_new = jnp.maximum(m, jnp.max(_fold_rows(s, jnp.maximum), axis=0, keepdims=True))
    alpha = jnp.exp(m - m_new)
    p = jnp.exp(s - m_new)
    l_new = alpha * l + jnp.sum(_fold_rows(p, jnp.add), axis=0, keepdims=True)
    return m_new, l_new, alpha, p.astype(BF16)


def _causal_split(qi, tq, ck):
    return (qi * tq + 1) // ck, ((qi + 1) * tq + ck - 1) // ck


CHUNK_STEPS = (4, 2, 1)


def _attention_loops(n_plain, n_chunks, tq, scores, adjust, edit_last, values):
    def consume(j, ss, carry, edit):
        stats, ps = [], []
        for h in range(N_HEADS):
            s = adjust(j, ss[h])
            if edit:
                s = edit_last(j, s)
            m_new, l_new, alpha, p = _softmax_chunk(s, carry[h][0], carry[h][1])
            stats.append((m_new, l_new, alpha))
            ps.append(p)
        out = []
        for h in range(N_HEADS):
            m_new, l_new, alpha = stats[h]
            out.append((m_new, l_new, alpha * carry[h][2] + _dot(values(j, h), ps[h])))
        return out

    def step(j0, carry, n, edit):
        ss = [scores(j0 + c) for c in range(n)]
        carry = list(carry)
        for c in range(n):
            carry = consume(j0 + c, ss[c], carry, edit)
        return tuple(carry)

    carry = tuple((jnp.full((1, tq), NEG_INF, F32), jnp.zeros((1, tq), F32),
                   jnp.zeros((HEAD_DIM, tq), F32)) for _ in range(N_HEADS))
    done = 0
    for width in CHUNK_STEPS:
        n_steps = (n_plain - done) // width
        carry = lax.fori_loop(
            0, n_steps, lambda i, c, w=width, d0=done: step(d0 + i * w, c, w, False), carry)
        done = done + n_steps * width
    if edit_last is None:
        return carry
    return lax.fori_loop(n_plain, n_chunks, lambda j, c: step(j, c, 1, True), carry)


def _fox_kernel(q_ref, k_ref, vt_ref, o_ref, *, tq, ck):
    qi = pl.program_id(1)
    n_full, n_chunks = _causal_split(qi, tq, ck)
    qpos = qi * tq + lax.broadcasted_iota(I32, (1, tq), 1)
    kiota = lax.broadcasted_iota(I32, (ck, 1), 0)
    qs = [q_ref[0, h] for h in range(N_HEADS)]

    def scores(j):
        start = pl.multiple_of(j * ck, ck)
        return [_dot_nt(k_ref[0, h, pl.ds(start, ck), :], qs[h]) for h in range(N_HEADS)]

    def causal(j, s):
        return jnp.where(j * ck + kiota <= qpos, s, NEG_INF)

    def values(j, h):
        return vt_ref[0, h, j]

    carry = _attention_loops(n_full, n_chunks, tq, scores, lambda j, s: s, causal, values)
    for h in range(N_HEADS):
        _, l, acc = carry[h]
        o_ref[0, h * HEAD_DIM:(h + 1) * HEAD_DIM, :] = (acc / l).astype(BF16)


def _fox_attention(fq, fk, fv, c_parts, tq, ck):
    b, nh, s, dh = fq.shape
    n_ck = s // ck
    pad = jnp.zeros((b, nh, s, AUG - dh - 3), BF16)
    q_aug = jnp.concatenate([fq, jnp.full((b, nh, s, 3), -1.0, BF16), pad], axis=-1)
    k_aug = jnp.concatenate([fk, jnp.moveaxis(c_parts, 1, 3), pad], axis=-1)
    v_t = jnp.swapaxes(fv.reshape(b, nh, n_ck, ck, dh), 3, 4)
    return pl.pallas_call(
        functools.partial(_fox_kernel, tq=tq, ck=ck),
        grid=(b, s // tq),
        in_specs=[pl.BlockSpec((1, nh, tq, AUG), lambda bi, i: (bi, 0, i, 0)),
                  pl.BlockSpec((1, nh, s, AUG), lambda bi, i: (bi, 0, 0, 0)),
                  pl.BlockSpec((1, nh, n_ck, dh, ck), lambda bi, i: (bi, 0, 0, 0, 0))],
        out_specs=pl.BlockSpec((1, nh * dh, tq), lambda bi, i: (bi, 0, i)),
        out_shape=jax.ShapeDtypeStruct((b, nh * dh, s), BF16),
        compiler_params=_cparams("parallel", "arbitrary"),
        name="fox_attention",
    )(q_aug, k_aug, v_t)


def _key_to_f32(key):
    bits = key ^ (lax.shift_right_arithmetic(key, 31) & 0x7FFFFFFF)
    return lax.bitcast_convert_type(bits, F32)


def _f32_to_key(x):
    bits = lax.bitcast_convert_type(x, I32)
    return bits ^ (lax.shift_right_arithmetic(bits, 31) & 0x7FFFFFFF)


KEY_POS_INF = 0x7F800000
KEY_NEG_INF = 0x007FFFFF - 2 ** 31
PROBE_STEP = 1 << 24
MAX_PROBES = 36
PROBES_PER_CHECK = 3


def _dsa_kernel(dq_ref, iq_ref, iwt_ref, ik_ref, dk_ref, dvt_ref, o_ref, score_buf, bias_buf, *,
                tq, ck, topk):
    qi = pl.program_id(1)
    n_full, n_chunks = _causal_split(qi, tq, ck)
    qpos = qi * tq + lax.broadcasted_iota(I32, (1, tq), 1)
    kiota = lax.broadcasted_iota(I32, (ck, 1), 0)
    w_idx = iwt_ref[0]
    kf = float(topk)

    def count(pred):
        def cb(j, acc):
            return acc + _fold_rows(jnp.where(pred(score_buf[j]), 1.0, 0.0), jnp.add)
        acc = lax.fori_loop(0, n_chunks, cb, jnp.zeros((SUBLANES, tq), F32))
        return jnp.sum(acc, axis=0, keepdims=True)

    def score_body(j, stats, masked):
        top, n_pos, n_nn = stats
        start = pl.multiple_of(j * ck, ck)
        ikc = ik_ref[0, pl.ds(start, ck), :]
        sc = None
        for h in range(N_HEADS):
            term = jnp.maximum(_dot_nt(ikc, iq_ref[0, h]), 0.0) * w_idx[h:h + 1, :]
            sc = term if sc is None else sc + term
        if masked:
            sc = jnp.where(start + kiota <= qpos, sc, NEG_INF)
        score_buf[j] = sc
        return (jnp.maximum(top, _fold_rows(sc, jnp.maximum)),
                n_pos + _fold_rows(jnp.where(sc > 0.0, 1.0, 0.0), jnp.add),
                n_nn + _fold_rows(jnp.where(sc >= 0.0, 1.0, 0.0), jnp.add))

    stats = (jnp.full((SUBLANES, tq), -jnp.inf, F32), jnp.zeros((SUBLANES, tq), F32),
             jnp.zeros((SUBLANES, tq), F32))
    stats = lax.fori_loop(0, n_full, functools.partial(score_body, masked=False), stats)
    stats = lax.fori_loop(n_full, n_chunks, functools.partial(score_body, masked=True), stats)

    top_key = _f32_to_key(jnp.max(stats[0], axis=0, keepdims=True))
    c_pos = jnp.sum(stats[1], axis=0, keepdims=True)
    c_nn = jnp.sum(stats[2], axis=0, keepdims=True)
    pos_side, zero_tie = c_pos >= kf, c_nn >= kf
    n_tot = (n_chunks * ck).astype(F32)
    lo = jnp.where(pos_side, 1, jnp.where(zero_tie, 0, KEY_NEG_INF)).astype(I32)
    hi = jnp.where(pos_side, KEY_POS_INF, jnp.where(zero_tie, 1, 0)).astype(I32)
    c_lo = jnp.where(pos_side, c_pos, jnp.where(zero_tie, c_nn, n_tot))

    def unsettled(lo, hi, c_lo):
        return (c_lo != kf) & (hi > lo + 1)

    def probe(state):
        it, lo, hi, c_lo = state
        mid = (lo >> 1) + (hi >> 1) + (lo & hi & 1)
        near_top = jnp.where(hi > 0, jnp.maximum(mid, hi - PROBE_STEP), mid)
        cand = jnp.where(it == 0, top_key, jnp.where(it == 1, near_top, mid))
        cand = jnp.minimum(jnp.maximum(cand, lo + 1), hi - 1)
        cand_f = _key_to_f32(cand)
        c = count(lambda s: s >= cand_f)
        live = unsettled(lo, hi, c_lo)
        up = live & (c >= kf)
        down = live & (c < kf)
        return (it + 1, jnp.where(up, cand, lo), jnp.where(down, cand, hi),
                jnp.where(up, c, c_lo))

    def searching(state):
        it, lo, hi, c_lo = state
        left = jnp.max(jnp.where(unsettled(lo, hi, c_lo), 1.0, 0.0))
        return (it < MAX_PROBES) & (left > 0.0)

    def probes(state):
        for _ in range(PROBES_PER_CHECK):
            state = probe(state)
        return state

    _, lo, _, _ = lax.while_loop(searching, probes, (jnp.int32(0), lo, hi, c_lo))
    thr = _key_to_f32(lo)

    need = kf - count(lambda s: s > thr)
    r_i = lax.broadcasted_iota(I32, (ck, ck), 0)
    c_i = lax.broadcasted_iota(I32, (ck, ck), 1)
    lower = jnp.where(c_i <= r_i, 1.0, 0.0).astype(BF16)

    def bias_body(j, seen, masked):
        start = pl.multiple_of(j * ck, ck)
        sc = score_buf[j]
        tie = sc == thr
        rank = _dot(lower, jnp.where(tie, 1.0, 0.0).astype(BF16)) + seen
        keep = jnp.where(sc > thr, 0.0, jnp.where(tie, jnp.where(rank <= need, 0.0, NEG_INF),
                                                  NEG_INF))
        if masked:
            keep = jnp.where(start + kiota <= qpos, keep, NEG_INF)
        bias_buf[j] = keep
        return rank[ck - 1:ck, :]

    seen = lax.fori_loop(0, n_full, functools.partial(bias_body, masked=False),
                         jnp.zeros((1, tq), F32))
    lax.fori_loop(n_full, n_chunks, functools.partial(bias_body, masked=True), seen)

    qs = [dq_ref[0, h] for h in range(N_HEADS)]

    def scores(j):
        start = pl.multiple_of(j * ck, ck)
        kc = dk_ref[0, pl.ds(start, ck), :]
        bias = bias_buf[j]
        return [_dot_nt(kc, qs[h]) + bias for h in range(N_HEADS)]

    carry = _attention_loops(n_chunks, n_chunks, tq, scores, lambda j, s: s, None,
                             lambda j, h: dvt_ref[0, j])
    for h in range(N_HEADS):
        _, l, acc = carry[h]
        o_ref[0, h * HEAD_DIM:(h + 1) * HEAD_DIM, :] = (acc / l).astype(BF16)


def _alibi_columns(s):
    pos = jnp.arange(s, dtype=I32)
    hi, lo = (pos // 64).astype(F32), (pos % 64).astype(F32)
    one = jnp.ones((s,), F32)
    k_cols = jnp.stack([hi, lo, one, one], axis=-1)
    q_cols = []
    for h in range(N_HEADS):
        slope = 2.0 ** (-8.0 * (h + 1) / N_HEADS)
        q_cols.append(jnp.stack([64.0 * slope * one, slope * one, -64.0 * slope * hi, -slope * lo],
                                axis=-1))
    return k_cols.astype(BF16), jnp.stack(q_cols, axis=0).astype(BF16)


def _dsa_attention(dq, iq, iw_t, ik, dk, dv, tq, ck):
    b, nh, s, dh = dq.shape
    assert s <= 64 * 64, "position columns hold two base-64 digits"
    topk = min(DSA_TOPK, s // 4)
    assert ck >= topk, "every query must see at least topk (possibly masked) candidates"
    n_ck = s // ck
    k_cols, q_cols = _alibi_columns(s)
    q_aug = jnp.concatenate([dq, jnp.broadcast_to(q_cols[None], (b, nh, s, 4)),
                             jnp.zeros((b, nh, s, AUG - dh - 4), BF16)], axis=-1)
    k_aug = jnp.concatenate([dk, jnp.broadcast_to(k_cols[None], (b, s, 4)),
                             jnp.zeros((b, s, AUG - dh - 4), BF16)], axis=-1)
    v_t = jnp.swapaxes(dv.reshape(b, n_ck, ck, dh), 2, 3)
    return pl.pallas_call(
        functools.partial(_dsa_kernel, tq=tq, ck=ck, topk=topk),
        grid=(b, s // tq),
        in_specs=[pl.BlockSpec((1, nh, tq, AUG), lambda bi, i: (bi, 0, i, 0)),
                  pl.BlockSpec((1, nh, tq, dh), lambda bi, i: (bi, 0, i, 0)),
                  pl.BlockSpec((1, 8, tq), lambda bi, i: (bi, 0, i)),
                  pl.BlockSpec((1, s, dh), lambda bi, i: (bi, 0, 0)),
                  pl.BlockSpec((1, s, AUG), lambda bi, i: (bi, 0, 0)),
                  pl.BlockSpec((1, n_ck, dh, ck), lambda bi, i: (bi, 0, 0, 0))],
        out_specs=pl.BlockSpec((1, nh * dh, tq), lambda bi, i: (bi, 0, i)),
        out_shape=jax.ShapeDtypeStruct((b, nh * dh, s), BF16),
        scratch_shapes=[pltpu.VMEM((n_ck, ck, tq), F32), pltpu.VMEM((n_ck, ck, tq), F32)],
        compiler_params=_cparams("parallel", "arbitrary"),
        name="dsa_attention",
    )(q_aug, iq, iw_t, ik, k_aug, v_t)


def _merge_kernel(h_ref, a_ref, b_ref, c_ref, d_ref, wg_ref, bg_ref, wb_ref, wo_ref, g_ref,
                  beta_ref, o_ref, *, alpha):
    h = h_ref[...]
    hb = h.astype(BF16)
    d = h.shape[1]
    acc = jnp.zeros(h.shape, F32)
    for n, br in enumerate((a_ref, b_ref, c_ref, d_ref)):
        gate = _sigmoid(_dot(hb, wg_ref[:, n * d:(n + 1) * d]) + bg_ref[:, n * d:(n + 1) * d])
        acc = acc + gate * _dot(br[...], wb_ref[n])
    mixed = _dot(acc.astype(BF16), wo_ref[...])
    o_ref[...] = _ln_rows(alpha * h + mixed, g_ref[...], beta_ref[...])


def _merge(h2d, branches, w_gate, b_gate, w_branch, w_out, ln_g, ln_b, alpha, tm):
    n, d = h2d.shape
    c = branches[0].shape[1]
    rows = lambda w: pl.BlockSpec((tm, w), lambda i: (i, 0))
    const = lambda shp: pl.BlockSpec(shp, lambda i: (0,) * len(shp))
    return pl.pallas_call(
        functools.partial(_merge_kernel, alpha=alpha),
        grid=(n // tm,),
        in_specs=[rows(d), rows(c), rows(c), rows(c), rows(c), const((d, N_BRANCH * d)),
                  const((1, N_BRANCH * d)), const((N_BRANCH, c, d)), const((d, d)),
                  const((1, d)), const((1, d))],
        out_specs=rows(d),
        out_shape=jax.ShapeDtypeStruct((n, d), F32),
        compiler_params=_cparams("parallel"),
        name="merge",
    )(h2d, *branches, w_gate, b_gate.reshape(1, -1), w_branch, w_out, ln_g.reshape(1, d),
      ln_b.reshape(1, d))


ROUTER_E0 = N_GROUPS
MOE_BLK = 256


def _routing_t(logits):
    row = lax.broadcasted_iota(I32, logits.shape, 0)
    far = jnp.int32(4 * LANES)
    neg = -jnp.inf

    def first_max(x):
        mx = jnp.max(x, axis=0, keepdims=True)
        idx = jnp.min(jnp.where(x == mx, row, far), axis=0, keepdims=True)
        return mx, idx

    lg = jnp.where(row < N_GROUPS, logits, neg)
    g_max, g_idx = first_max(lg)
    p_group = 1.0 / jnp.sum(jnp.exp(lg - g_max), axis=0, keepdims=True)
    lo = ROUTER_E0 + EXPERTS_PER_GROUP * g_idx
    le = jnp.where((row >= lo) & (row < lo + EXPERTS_PER_GROUP), logits, neg)
    m1, e1 = first_max(le)
    m2, e2 = first_max(jnp.where(row == e1, neg, le))
    r = jnp.exp(m2 - m1)
    gate1 = p_group * (1.0 / (1.0 + r))
    gate2 = p_group * (r / (1.0 + r))
    return jnp.where(row == e1, gate1, jnp.where(row == e2, gate2, 0.0)), g_idx


def _bf16_pieces(x, n):
    out, rest = [], x
    for _ in range(n):
        part = rest.astype(BF16)
        out.append(part)
        rest = rest - part.astype(F32)
    return out


def _moe_kernel(h_ref, wr_ref, br_ref, wgu_ref, wd_ref, g_ref, beta_ref, o_ref, blk_ref, p_ref,
                xs_ref, gs_ref, acc_ref, *, alpha):
    e = pl.program_id(1)
    t, d = h_ref.shape
    f = wd_ref.shape[2]
    n_slots = p_ref.shape[0] // MOE_BLK

    @pl.when(e == 0)
    def _():
        h = h_ref[...]
        hb = h.astype(BF16)
        gates, g_idx = _routing_t(_dot_nt(wr_ref[...], hb) + br_ref[...])
        grp = lax.broadcasted_iota(I32, (SUBLANES, t), 0)
        onehot = jnp.where(grp == g_idx, 1.0, 0.0)
        r_i = lax.broadcasted_iota(I32, (t, t), 0)
        c_i = lax.broadcasted_iota(I32, (t, t), 1)
        before = jnp.where(r_i < c_i, 1.0, 0.0).astype(BF16)
        rank = _dot(onehot.astype(BF16), before)
        total = jnp.sum(onehot, axis=1, keepdims=True)
        start = jnp.int32(0)
        starts = jnp.zeros((SUBLANES, 1), F32)
        grp_col = lax.broadcasted_iota(I32, (SUBLANES, 1), 0)
        for g in range(N_GROUPS):
            n_g = (total[g, 0].astype(I32) + (MOE_BLK - 1)) // MOE_BLK
            blk_ref[2 * g] = start
            blk_ref[2 * g + 1] = n_g
            starts = jnp.where(grp_col == g, (start * MOE_BLK).astype(F32), starts)
            start = start + n_g
        blk_ref[2 * N_GROUPS] = start
        pos = jnp.sum(onehot * (starts + rank), axis=0, keepdims=True).astype(I32)
        slot = lax.broadcasted_iota(I32, (p_ref.shape[0], t), 0)
        p_ref[...] = jnp.where(slot == pos, 1.0, 0.0).astype(BF16)
        perm = p_ref[...]
        xs = _dot_nt(h.T.astype(BF16), perm).astype(BF16)
        gs = sum(_dot_nt(piece, perm) for piece in _bf16_pieces(gates, 3))
        for b in range(n_slots):
            cols = slice(b * MOE_BLK, (b + 1) * MOE_BLK)
            xs_ref[b] = xs[:, cols]
            acc_ref[b] = jnp.zeros((d, MOE_BLK), F32)
            for k in range(N_EXPERTS):
                gs_ref[b, k] = gs[ROUTER_E0 + k:ROUTER_E0 + k + 1, cols]

    grp_e = e // EXPERTS_PER_GROUP
    blk0 = blk_ref[2 * grp_e]

    def expert_block(i, _):
        b = blk0 + i
        xs = xs_ref[b]
        hg = _dot(wgu_ref[0, :f], xs)
        hu = _dot(wgu_ref[0, f:], xs)
        hid = (hg * _sigmoid(hg) * hu).astype(BF16)
        y = jnp.concatenate([_dot(wd_ref[0, :d // 2], hid), _dot(wd_ref[0, d // 2:], hid)], axis=0)
        acc_ref[b] += gs_ref[b, e] * y
        return 0

    lax.fori_loop(0, blk_ref[2 * grp_e + 1], expert_block, 0)

    @pl.when(e == pl.num_programs(1) - 1)
    def _():
        pieces = [jnp.concatenate(col, axis=1) for col in
                  zip(*[_bf16_pieces(acc_ref[b], 2) for b in range(n_slots)])]
        half = t // 2
        for c in range(2):
            cols = slice(c * half, (c + 1) * half)
            out_t = sum(_dot(piece, p_ref[:, cols]) for piece in pieces)
            o_ref[cols, :] = _ln_rows(alpha * h_ref[cols, :] + out_t.T, g_ref[...], beta_ref[...])


def _moe(h2d, w_router_t, b_router_t, wgu_t, wd_t, ln_g, ln_b, alpha, tm):
    n, d = h2d.shape
    ne, _, f = wd_t.shape
    assert tm % MOE_BLK == 0
    n_slots = tm // MOE_BLK + N_GROUPS
    rows = pl.BlockSpec((tm, d), lambda i, e: (i, 0))
    const = lambda shp: pl.BlockSpec(shp, lambda i, e: (0,) * len(shp))
    return pl.pallas_call(
        functools.partial(_moe_kernel, alpha=alpha),
        grid=(n // tm, ne),
        in_specs=[rows, const((LANES, d)), const((LANES, 1)),
                  pl.BlockSpec((1, 2 * f, d), lambda i, e: (e, 0, 0)),
                  pl.BlockSpec((1, d, f), lambda i, e: (e, 0, 0)),
                  const((1, d)), const((1, d))],
        out_specs=rows,
        out_shape=jax.ShapeDtypeStruct((n, d), F32),
        scratch_shapes=[pltpu.SMEM((2 * N_GROUPS + 1,), I32),
                        pltpu.VMEM((n_slots * MOE_BLK, tm), BF16),
                        pltpu.VMEM((n_slots, d, MOE_BLK), BF16),
                        pltpu.VMEM((n_slots, ne, 1, MOE_BLK), F32),
                        pltpu.VMEM((n_slots, d, MOE_BLK), F32)],
        compiler_params=pltpu.CompilerParams(dimension_semantics=("parallel", "arbitrary"),
                                             vmem_limit_bytes=MOE_VMEM_LIMIT),
        name="moe",
    )(h2d, w_router_t, b_router_t, wgu_t, wd_t, ln_g.reshape(1, d), ln_b.reshape(1, d))


def _tiles(b, s):
    n = b * s
    pick = lambda total, want: want if total % want == 0 else total
    return dict(
        ln_tm=pick(n, 512), proj_tm=pick(s, 512), local_ts=pick(s, 512),
        fox_tq=pick(s, 256), fox_ck=pick(s, 512), dsa_tq=pick(s, 256), dsa_ck=pick(s, 512),
        merge_tm=pick(n, 512), moe_tm=pick(n, 1024))


def kernel(x, ln_in_g, ln_in_b, w_in, b_forget, b_gate, pool_w, pool_scale, conv_dw, conv_b,
           conv_ln_g, conv_ln_b, w_branch, w_out, ln1_g, ln1_b, router_g, router_g_b, router_e,
           router_e_b, expert_w_gate, expert_w_up, expert_w_down, ln2_g, ln2_b):
    b, s, d = x.shape
    depth = w_in.shape[0]
    alpha = (2.0 * depth) ** 0.25
    t = _tiles(b, s)
    n = b * s

    h = _layer_norm(x.reshape(n, d), ln_in_g, ln_in_b, t["ln_tm"])
    for l in range(depth):
        wp = _pack_proj_weight(w_in[l])
        (pool_v, fq, fk, fv, dq, iq, glu, dk, dv, ik, small) = _projection(
            h.reshape(b, s, d), wp, t["proj_tm"])

        small_t = jnp.swapaxes(small[..., SMALL_FF:SMALL_FF + 2 * N_HEADS], 1, 2)
        f_t = jnp.pad(small_t[:, :N_HEADS], ((0, 0), (0, 8 - N_HEADS), (0, 0)))
        iw_t = jnp.pad(small_t[:, N_HEADS:], ((0, 0), (0, 8 - N_HEADS), (0, 0)))
        bf_col = jnp.pad(b_forget[l], (0, 8 - N_HEADS)).reshape(8, 1)
        c_parts = _forget_cumsum(f_t, bf_col)[:, :, :N_HEADS]

        pool_wbd = jnp.zeros((BRANCH_W, BRANCH_W), F32)
        for g in range(len(POOL_WINDOWS)):
            sl = slice(g * POOL_GROUP, (g + 1) * POOL_GROUP)
            pool_wbd = pool_wbd.at[sl, sl].set(pool_w[l, g])
        br_a, br_d = _local_mixers(pool_v, glu, pool_wbd.astype(BF16), pool_scale[l], conv_dw[l],
                                   conv_b[l], conv_ln_g[l], conv_ln_b[l], t["local_ts"])
        br_b = _fox_attention(fq, fk, fv, c_parts, t["fox_tq"], t["fox_ck"])
        br_c = _dsa_attention(dq, iq, iw_t, ik, dk, dv, t["dsa_tq"], t["dsa_ck"])
        br_b, br_c = jnp.swapaxes(br_b, 1, 2), jnp.swapaxes(br_c, 1, 2)

        branches = [br.reshape(n, BRANCH_W) for br in (br_a, br_b, br_c, br_d)]
        h = _merge(h, branches, w_in[l][:, GATE_OFF:].astype(BF16), b_gate[l],
                   w_branch[l].astype(BF16), w_out[l].astype(BF16), ln1_g[l], ln1_b[l], alpha,
                   t["merge_tm"])

        w_router_t = jnp.concatenate(
            [router_g[l], router_e[l], jnp.zeros((d, LANES - N_GROUPS - N_EXPERTS), F32)],
            axis=1).T.astype(BF16)
        b_router_t = jnp.concatenate(
            [router_g_b[l], router_e_b[l], jnp.zeros((LANES - N_GROUPS - N_EXPERTS,), F32)]
        ).reshape(LANES, 1)
        wgu_t = jnp.concatenate([jnp.swapaxes(expert_w_gate[l], 1, 2),
                                 jnp.swapaxes(expert_w_up[l], 1, 2)], axis=1).astype(BF16)
        wd_t = jnp.swapaxes(expert_w_down[l], 1, 2).astype(BF16)
        h = _moe(h, w_router_t, b_router_t, wgu_t, wd_t, ln2_g[l], ln2_b[l], alpha, t["moe_tm"])
    return h.reshape(b, s, d)
```

```python
import functools

import jax
import jax.numpy as jnp
from jax import lax
from jax.experimental import pallas as pl
from jax.experimental.pallas import tpu as pltpu

F32 = jnp.float32
BF16 = jnp.bfloat16
I32 = jnp.int32

N_BRANCH = 4
HEAD_DIM = 64
N_HEADS = 4
BRANCH_W = N_HEADS * HEAD_DIM
POOL_WINDOWS = (2, 4, 8, 16)
POOL_GROUP = BRANCH_W // len(POOL_WINDOWS)
CONV_W = 31
DSA_TOPK = 256
N_GROUPS = 4
EXPERTS_PER_GROUP = 4
N_EXPERTS = N_GROUPS * EXPERTS_PER_GROUP
LN_EPS = 1e-5
NEG_INF = -1e30
ATTN_SCALE = HEAD_DIM ** -0.5

LANES = 128
HALO = 32
VMEM_LIMIT = 56 * 1024 * 1024
MOE_VMEM_LIMIT = 60 * 1024 * 1024

_OFF = {}
_o = 0
for _name, _n in (("pool", BRANCH_W), ("fq", BRANCH_W), ("fk", BRANCH_W), ("fv", BRANCH_W),
                  ("ff", N_HEADS), ("dq", BRANCH_W), ("dk", HEAD_DIM), ("dv", HEAD_DIM),
                  ("iq", BRANCH_W), ("ik", HEAD_DIM), ("iw", N_HEADS), ("conv", 2 * BRANCH_W)):
    _OFF[_name] = (_o, _o + _n)
    _o += _n
GATE_OFF = _o
SMALL_FF = HEAD_DIM
SMALL_IW = HEAD_DIM + N_HEADS
AUG = 2 * HEAD_DIM
POS_DIGIT_BITS = 6


def _cparams(*sem):
    return pltpu.CompilerParams(dimension_semantics=sem, vmem_limit_bytes=VMEM_LIMIT)


def _ln_rows(x, g, b):
    mu = jnp.mean(x, axis=-1, keepdims=True)
    xc = x - mu
    var = jnp.mean(xc * xc, axis=-1, keepdims=True)
    return xc * lax.rsqrt(var + LN_EPS) * g + b


def _sigmoid(x):
    return 1.0 / (1.0 + jnp.exp(-x))


def _dot(a, b):
    return jnp.dot(a, b, preferred_element_type=F32)


def _dot_nt(a, b):
    return lax.dot_general(a, b, (((1,), (1,)), ((), ())), preferred_element_type=F32)


def _ln_kernel(x_ref, g_ref, b_ref, o_ref):
    o_ref[...] = _ln_rows(x_ref[...], g_ref[...], b_ref[...])


def _layer_norm(x2d, g, b, tm):
    n, d = x2d.shape
    return pl.pallas_call(
        _ln_kernel,
        grid=(n // tm,),
        in_specs=[pl.BlockSpec((tm, d), lambda i: (i, 0)),
                  pl.BlockSpec((1, d), lambda i: (0, 0)),
                  pl.BlockSpec((1, d), lambda i: (0, 0))],
        out_specs=pl.BlockSpec((tm, d), lambda i: (i, 0)),
        out_shape=jax.ShapeDtypeStruct((n, d), F32),
        compiler_params=_cparams("parallel"),
        name="ln_in",
    )(x2d, g.reshape(1, d), b.reshape(1, d))


_PK = {"pool": 0, "fq": 256, "fk": 512, "fv": 768, "dq": 1024, "iq": 1280, "conv": 1536,
       "dkv": 2048, "small": 2176}
PK_COLS = 2304


def _pack_proj_weight(w):
    d = w.shape[0]
    sl = lambda k: w[:, _OFF[k][0]:_OFF[k][1]]
    small = jnp.concatenate([sl("ik"), sl("ff"), sl("iw"),
                             jnp.zeros((d, LANES - HEAD_DIM - 2 * N_HEADS), w.dtype)], axis=1)
    packed = jnp.concatenate([sl("pool"), sl("fq") * ATTN_SCALE, sl("fk"), sl("fv"),
                              sl("dq") * ATTN_SCALE, sl("iq"), sl("conv"), sl("dk"), sl("dv"),
                              small], axis=1)
    return packed.astype(BF16)


def _proj_kernel(h_ref, w_ref, pool_o, fq_o, fk_o, fv_o, dq_o, iq_o, glu_o, dk_o, dv_o, ik_o,
                 small_o):
    hb = h_ref[0].astype(BF16)
    tm = hb.shape[0]

    def piece(name, width):
        return _dot(hb, w_ref[:, _PK[name]:_PK[name] + width])

    lane = lax.broadcasted_iota(I32, (tm, HEAD_DIM), 1)
    pos = pl.program_id(1) * tm + lax.broadcasted_iota(I32, (tm, HEAD_DIM), 0)
    p_hi = (pos >> POS_DIGIT_BITS).astype(F32)
    p_lo = (pos & ((1 << POS_DIGIT_BITS) - 1)).astype(F32)
    digit = float(1 << POS_DIGIT_BITS)

    def bias_cols(c0, c1, c2, c3):
        cols = jnp.where(lane == 0, c0, jnp.where(lane == 1, c1, jnp.where(
            lane == 2, c2, jnp.where(lane == 3, c3, 0.0))))
        return cols.astype(BF16)

    def widen(r, cols):
        return jnp.concatenate([r, cols], axis=1)

    pool_o[0] = piece("pool", BRANCH_W)
    for name, out in (("fq", fq_o), ("fk", fk_o), ("fv", fv_o), ("dq", dq_o), ("iq", iq_o)):
        r = piece(name, BRANCH_W).astype(BF16)
        for h in range(N_HEADS):
            r_h = r[:, h * HEAD_DIM:(h + 1) * HEAD_DIM]
            if name == "fq":
                r_h = widen(r_h, bias_cols(-1.0, -1.0, -1.0, 0.0))
            elif name == "dq":
                slope = 2.0 ** (-8.0 * (h + 1) / N_HEADS)
                r_h = widen(r_h, bias_cols(digit * slope, slope, -digit * slope * p_hi,
                                           -slope * p_lo))
            out[0, h] = r_h
    a = piece("conv", 2 * BRANCH_W)
    glu_o[0] = a[:, :BRANCH_W] * _sigmoid(a[:, BRANCH_W:])
    kv = piece("dkv", 2 * HEAD_DIM).astype(BF16)
    dk_o[0] = widen(kv[:, :HEAD_DIM], bias_cols(p_hi, p_lo, 1.0, 1.0))
    dv_o[0] = kv[:, HEAD_DIM:]
    sm = piece("small", LANES)
    small_o[0] = sm
    ik_o[0] = sm[:, :HEAD_DIM].astype(BF16)


def _projection(h, wp, tm):
    b, s, d = h.shape
    assert s <= (1 << POS_DIGIT_BITS) ** 2, "positions are carried as two digits"
    heads = lambda w: pl.BlockSpec((1, N_HEADS, tm, w), lambda bi, i: (bi, 0, i, 0))
    rows = lambda w: pl.BlockSpec((1, tm, w), lambda bi, i: (bi, i, 0))
    head_shape = lambda w: jax.ShapeDtypeStruct((b, N_HEADS, s, w), BF16)
    hd, aug = HEAD_DIM, AUG
    return pl.pallas_call(
        _proj_kernel,
        grid=(b, s // tm),
        in_specs=[rows(d), pl.BlockSpec((d, PK_COLS), lambda bi, i: (0, 0))],
        out_specs=[rows(BRANCH_W), heads(aug), heads(hd), heads(hd), heads(aug), heads(hd),
                   rows(BRANCH_W), rows(aug), rows(hd), rows(hd), rows(LANES)],
        out_shape=[jax.ShapeDtypeStruct((b, s, BRANCH_W), F32), head_shape(aug), head_shape(hd),
                   head_shape(hd), head_shape(aug), head_shape(hd),
                   jax.ShapeDtypeStruct((b, s, BRANCH_W), F32),
                   jax.ShapeDtypeStruct((b, s, AUG), BF16),
                   jax.ShapeDtypeStruct((b, s, HEAD_DIM), BF16),
                   jax.ShapeDtypeStruct((b, s, HEAD_DIM), BF16),
                   jax.ShapeDtypeStruct((b, s, LANES), F32)],
        compiler_params=_cparams("parallel", "parallel"),
        name="proj",
    )(h, wp)


def _forget_cumsum_kernel(f_ref, b_ref, o_ref):
    s = f_ref.shape[2]
    x = f_ref[0] + b_ref[...]
    ls = jnp.minimum(x, 0.0) - jnp.log(1.0 + jnp.exp(-jnp.abs(x)))
    r = lax.broadcasted_iota(I32, (LANES, LANES), 0)
    c = lax.broadcasted_iota(I32, (LANES, LANES), 1)
    upper = jnp.where(r <= c, 1.0, 0.0).astype(F32)
    carry = jnp.zeros((x.shape[0], 1), F32)
    for j in range(s // LANES):
        blk = jnp.dot(ls[:, j * LANES:(j + 1) * LANES], upper, preferred_element_type=F32,
                      precision=lax.Precision.HIGHEST) + carry
        carry = blk[:, LANES - 1:LANES]
        rest = blk
        for piece in range(3):
            part = rest.astype(BF16)
            o_ref[0, piece, :, j * LANES:(j + 1) * LANES] = part
            rest = rest - part.astype(F32)


def _forget_cumsum(f_t, b_col):
    b, r, s = f_t.shape
    return pl.pallas_call(
        _forget_cumsum_kernel,
        grid=(b,),
        in_specs=[pl.BlockSpec((1, r, s), lambda bi: (bi, 0, 0)),
                  pl.BlockSpec((r, 1), lambda bi: (0, 0))],
        out_specs=pl.BlockSpec((1, 3, r, s), lambda bi: (bi, 0, 0, 0)),
        out_shape=jax.ShapeDtypeStruct((b, 3, r, s), BF16),
        compiler_params=_cparams("parallel"),
        name="forget_cumsum",
    )(f_t, b_col)


def _local_kernel(pv_ref, pvh_ref, gl_ref, glh_ref, pw_ref, ps_ref, dw_ref, cb_ref, g_ref, b_ref,
                  a_o, d_o, pbuf, cbuf, *, ts):
    i = pl.program_id(1)
    first = i == 0
    pbuf[0:HALO, :] = jnp.where(first, 0.0, pvh_ref[0])
    pbuf[HALO:HALO + ts, :] = pv_ref[0]
    cbuf[0:HALO, :] = jnp.where(first, 0.0, glh_ref[0])
    cbuf[HALO:HALO + ts, :] = gl_ref[0]

    lane = lax.broadcasted_iota(I32, (ts, BRANCH_W), 1)
    pos = (i * ts + 1 + lax.broadcasted_iota(I32, (ts, 1), 0)).astype(F32)
    v = pbuf[HALO:HALO + ts, :]
    run = v
    pooled = jnp.zeros((ts, BRANCH_W), F32)
    shift = 1
    for g, w in enumerate(POOL_WINDOWS):
        while shift < w:
            run = run + pbuf[HALO - shift:HALO - shift + ts, :]
            shift += 1
        in_group = (lane >= g * POOL_GROUP) & (lane < (g + 1) * POOL_GROUP)
        pooled = jnp.where(in_group, run / jnp.minimum(pos, float(w)), pooled)
    pooled = pooled - v
    a_o[0] = (_dot(pooled.astype(BF16), pw_ref[...]) * ps_ref[...]).astype(BF16)

    y = jnp.zeros((ts, BRANCH_W), F32) + cb_ref[...]
    for j in range(CONV_W):
        back = CONV_W - 1 - j
        y = y + cbuf[HALO - back:HALO - back + ts, :] * dw_ref[j:j + 1, :]
    z = _ln_rows(y, g_ref[...], b_ref[...])
    d_o[0] = (z * _sigmoid(z)).astype(BF16)


def _local_mixers(pool_v, glu, pool_wbd, pool_scale, conv_dw, conv_b, ln_g, ln_b, ts):
    b, s, c = pool_v.shape
    cur = pl.BlockSpec((1, ts, c), lambda bi, i: (bi, i, 0))
    halo = pl.BlockSpec((1, HALO, c), lambda bi, i: (bi, jnp.maximum(i * (ts // HALO) - 1, 0), 0))
    full2 = lambda shp: pl.BlockSpec(shp, lambda bi, i: (0, 0))
    out_shape = jax.ShapeDtypeStruct((b, s, c), BF16)
    return pl.pallas_call(
        functools.partial(_local_kernel, ts=ts),
        grid=(b, s // ts),
        in_specs=[cur, halo, cur, halo, full2((c, c)), full2((1, c)), full2((CONV_W, c)),
                  full2((1, c)), full2((1, c)), full2((1, c))],
        out_specs=[cur, cur],
        out_shape=[out_shape, out_shape],
        scratch_shapes=[pltpu.VMEM((HALO + ts, c), F32), pltpu.VMEM((HALO + ts, c), F32)],
        compiler_params=_cparams("parallel", "parallel"),
        name="local_mixers",
    )(pool_v, pool_v, glu, glu, pool_wbd, pool_scale.reshape(1, c), conv_dw, conv_b.reshape(1, c),
      ln_g.reshape(1, c), ln_b.reshape(1, c))


SUBLANES = 8
FOLD_CHAINS = 8


def _fold_rows(x, op):
    slabs = [x[i:i + SUBLANES] for i in range(0, x.shape[0], SUBLANES)]
    accs = slabs[:FOLD_CHAINS]
    for i, slab in enumerate(slabs[FOLD_CHAINS:]):
        accs[i % FOLD_CHAINS] = op(accs[i % FOLD_CHAINS], slab)
    while len(accs) > 1:
        accs = [op(accs[i], accs[i + 1]) for i in range(0, len(accs) - 1, 2)] + (
            [accs[-1]] if len(accs) % 2 else [])
    return accs[0]


def _softmax_chunk(s, m, l):
    m_new = jnp.maximum(m, jnp.max(_fold_rows(s, jnp.maximum), axis=0, keepdims=True))
    alpha = jnp.exp(m - m_new)
    p = jnp.exp(s - m_new)
    l_new = alpha * l + jnp.sum(_fold_rows(p, jnp.add), axis=0, keepdims=True)
    return m_new, l_new, alpha, p.astype(BF16)


def _causal_split(qi, tq, ck):
    return (qi * tq + 1) // ck, ((qi + 1) * tq + ck - 1) // ck


CHUNK_STEPS = (4, 2, 1)


def _attention_loops(n_plain, n_chunks, tq, scores, adjust, edit_last, values):
    def consume(j, ss, carry, edit):
        stats, ps = [], []
        for h in range(N_HEADS):
            s = adjust(j, ss[h])
            if edit:
                s = edit_last(j, s)
            m_new, l_new, alpha, p = _softmax_chunk(s, carry[h][0], carry[h][1])
            stats.append((m_new, l_new, alpha))
            ps.append(p)
        out = []
        for h in range(N_HEADS):
            m_new, l_new, alpha = stats[h]
            out.append((m_new, l_new, alpha * carry[h][2] + _dot(values(j, h), ps[h])))
        return out

    def step(j0, carry, n, edit):
        ss = [scores(j0 + c) for c in range(n)]
        carry = list(carry)
        for c in range(n):
            carry = consume(j0 + c, ss[c], carry, edit)
        return tuple(carry)

    carry = tuple((jnp.full((1, tq), NEG_INF, F32), jnp.zeros((1, tq), F32),
                   jnp.zeros((HEAD_DIM, tq), F32)) for _ in range(N_HEADS))
    done = 0
    for width in CHUNK_STEPS:
        n_steps = (n_plain - done) // width
        carry = lax.fori_loop(
            0, n_steps, lambda i, c, w=width, d0=done: step(d0 + i * w, c, w, False), carry)
        done = done + n_steps * width
    if edit_last is None:
        return carry
    return lax.fori_loop(n_plain, n_chunks, lambda j, c: step(j, c, 1, True), carry)


def _fox_kernel(q_ref, k_ref, vt_ref, o_ref, *, tq, ck):
    qi = pl.program_id(1)
    n_full, n_chunks = _causal_split(qi, tq, ck)
    qpos = qi * tq + lax.broadcasted_iota(I32, (1, tq), 1)
    kiota = lax.broadcasted_iota(I32, (ck, 1), 0)
    qs = [q_ref[0, h] for h in range(N_HEADS)]

    def scores(j):
        start = pl.multiple_of(j * ck, ck)
        return [_dot_nt(k_ref[0, h, pl.ds(start, ck), :], qs[h]) for h in range(N_HEADS)]

    def causal(j, s):
        return jnp.where(j * ck + kiota <= qpos, s, NEG_INF)

    def values(j, h):
        return vt_ref[0, h, j]

    carry = _attention_loops(n_full, n_chunks, tq, scores, lambda j, s: s, causal, values)
    for h in range(N_HEADS):
        _, l, acc = carry[h]
        o_ref[0, h * HEAD_DIM:(h + 1) * HEAD_DIM, :] = (acc / l).astype(BF16)


def _fox_attention(q_aug, fk, fv, c_parts, tq, ck):
    b, nh, s, dh = fk.shape
    n_ck = s // ck
    pad = jnp.zeros((b, nh, s, AUG - dh - 3), BF16)
    k_aug = jnp.concatenate([fk, jnp.moveaxis(c_parts, 1, 3), pad], axis=-1)
    v_t = jnp.swapaxes(fv.reshape(b, nh, n_ck, ck, dh), 3, 4)
    return pl.pallas_call(
        functools.partial(_fox_kernel, tq=tq, ck=ck),
        grid=(b, s // tq),
        in_specs=[pl.BlockSpec((1, nh, tq, AUG), lambda bi, i: (bi, 0, i, 0)),
                  pl.BlockSpec((1, nh, s, AUG), lambda bi, i: (bi, 0, 0, 0)),
                  pl.BlockSpec((1, nh, n_ck, dh, ck), lambda bi, i: (bi, 0, 0, 0, 0))],
        out_specs=pl.BlockSpec((1, nh * dh, tq), lambda bi, i: (bi, 0, i)),
        out_shape=jax.ShapeDtypeStruct((b, nh * dh, s), BF16),
        compiler_params=_cparams("parallel", "arbitrary"),
        name="fox_attention",
    )(q_aug, k_aug, v_t)


def _key_to_f32(key):
    bits = key ^ (lax.shift_right_arithmetic(key, 31) & 0x7FFFFFFF)
    return lax.bitcast_convert_type(bits, F32)


def _f32_to_key(x):
    bits = lax.bitcast_convert_type(x, I32)
    return bits ^ (lax.shift_right_arithmetic(bits, 31) & 0x7FFFFFFF)


KEY_POS_INF = 0x7F800000
KEY_MIN_NORMAL = 0x00800000
KEY_NEG_INF = 0x007FFFFF - 2 ** 31
PROBE_STEP = 1 << 24
MAX_PROBES = 36
PROBES_PER_CHECK = 3


def _dsa_kernel(dq_ref, iq_ref, iwt_ref, ik_ref, dk_ref, dvt_ref, o_ref, score_buf, bias_buf, *,
                tq, ck, topk):
    qi = pl.program_id(1)
    n_full, n_chunks = _causal_split(qi, tq, ck)
    qpos = qi * tq + lax.broadcasted_iota(I32, (1, tq), 1)
    kiota = lax.broadcasted_iota(I32, (ck, 1), 0)
    w_idx = iwt_ref[0]
    kf = float(topk)

    def count(pred):
        def cb(j, acc):
            return acc + _fold_rows(jnp.where(pred(score_buf[j]), 1.0, 0.0), jnp.add)
        acc = lax.fori_loop(0, n_chunks, cb, jnp.zeros((SUBLANES, tq), F32))
        return jnp.sum(acc, axis=0, keepdims=True)

    def score_body(j, stats, masked):
        top, n_pos, n_nn = stats
        start = pl.multiple_of(j * ck, ck)
        ikc = ik_ref[0, pl.ds(start, ck), :]
        sc = None
        for h in range(N_HEADS):
            term = jnp.maximum(_dot_nt(ikc, iq_ref[0, h]), 0.0) * w_idx[h:h + 1, :]
            sc = term if sc is None else sc + term
        if masked:
            sc = jnp.where(start + kiota <= qpos, sc, NEG_INF)
        score_buf[j] = sc
        return (jnp.maximum(top, _fold_rows(sc, jnp.maximum)),
                n_pos + _fold_rows(jnp.where(sc > 0.0, 1.0, 0.0), jnp.add),
                n_nn + _fold_rows(jnp.where(sc >= 0.0, 1.0, 0.0), jnp.add))

    stats = (jnp.full((SUBLANES, tq), -jnp.inf, F32), jnp.zeros((SUBLANES, tq), F32),
             jnp.zeros((SUBLANES, tq), F32))
    stats = lax.fori_loop(0, n_full, functools.partial(score_body, masked=False), stats)
    stats = lax.fori_loop(n_full, n_chunks, functools.partial(score_body, masked=True), stats)

    top_key = _f32_to_key(jnp.max(stats[0], axis=0, keepdims=True))
    c_pos = jnp.sum(stats[1], axis=0, keepdims=True)
    c_nn = jnp.sum(stats[2], axis=0, keepdims=True)
    pos_side, zero_tie = c_pos >= kf, c_nn >= kf
    n_tot = (n_chunks * ck).astype(F32)
    lo = jnp.where(pos_side, KEY_MIN_NORMAL, jnp.where(zero_tie, 0, KEY_NEG_INF)).astype(I32)
    hi = jnp.where(pos_side, KEY_POS_INF, jnp.where(zero_tie, 1, 0)).astype(I32)
    c_lo = jnp.where(pos_side, c_pos, jnp.where(zero_tie, c_nn, n_tot))
    c_hi = jnp.where(pos_side, 0.0, jnp.where(zero_tie, c_pos, c_nn))

    def unsettled(lo, hi, c_lo):
        return (c_lo != kf) & (hi > lo + 1)

    def probe(state):
        it, lo, hi, c_lo, c_hi = state
        mid = (lo >> 1) + (hi >> 1) + (lo & hi & 1)
        near_top = jnp.where(hi > 0, jnp.maximum(mid, hi - PROBE_STEP), mid)
        cand = jnp.where(it == 0, top_key, jnp.where(it == 1, near_top, mid))
        cand = jnp.minimum(jnp.maximum(cand, lo + 1), hi - 1)
        cand_f = _key_to_f32(cand)
        c = count(lambda s: s >= cand_f)
        live = unsettled(lo, hi, c_lo)
        up = live & (c >= kf)
        down = live & (c < kf)
        return (it + 1, jnp.where(up, cand, lo), jnp.where(down, cand, hi),
                jnp.where(up, c, c_lo), jnp.where(down, c, c_hi))

    def searching(state):
        it, lo, hi, c_lo, _ = state
        left = jnp.max(jnp.where(unsettled(lo, hi, c_lo), 1.0, 0.0))
        return (it < MAX_PROBES) & (left > 0.0)

    def probes(state):
        for _ in range(PROBES_PER_CHECK):
            state = probe(state)
        return state

    _, lo, hi, _, c_hi = lax.while_loop(searching, probes, (jnp.int32(0), lo, hi, c_lo, c_hi))
    thr = _key_to_f32(lo)

    need = jnp.where(hi == lo + 1, kf - c_hi, float(ck) * n_tot)
    r_i = lax.broadcasted_iota(I32, (ck, ck), 0)
    c_i = lax.broadcasted_iota(I32, (ck, ck), 1)
    lower = jnp.where(c_i <= r_i, 1.0, 0.0).astype(BF16)

    def bias_body(j, seen, masked):
        start = pl.multiple_of(j * ck, ck)
        sc = score_buf[j]
        tie = sc == thr
        rank = _dot(lower, jnp.where(tie, 1.0, 0.0).astype(BF16)) + seen
        keep = jnp.where(sc > thr, 0.0, jnp.where(tie, jnp.where(rank <= need, 0.0, NEG_INF),
                                                  NEG_INF))
        if masked:
            keep = jnp.where(start + kiota <= qpos, keep, NEG_INF)
        bias_buf[j] = keep
        return rank[ck - 1:ck, :]

    seen = lax.fori_loop(0, n_full, functools.partial(bias_body, masked=False),
                         jnp.zeros((1, tq), F32))
    lax.fori_loop(n_full, n_chunks, functools.partial(bias_body, masked=True), seen)

    qs = [dq_ref[0, h] for h in range(N_HEADS)]

    def scores(j):
        start = pl.multiple_of(j * ck, ck)
        kc = dk_ref[0, pl.ds(start, ck), :]
        bias = bias_buf[j]
        return [_dot_nt(kc, qs[h]) + bias for h in range(N_HEADS)]

    carry = _attention_loops(n_chunks, n_chunks, tq, scores, lambda j, s: s, None,
                             lambda j, h: dvt_ref[0, j])
    for h in range(N_HEADS):
        _, l, acc = carry[h]
        o_ref[0, h * HEAD_DIM:(h + 1) * HEAD_DIM, :] = (acc / l).astype(BF16)


def _dsa_attention(q_aug, iq, iw_t, ik, k_aug, dv, tq, ck):
    b, nh, s, dh = iq.shape
    topk = min(DSA_TOPK, s // 4)
    assert ck >= topk, "every query must see at least topk (possibly masked) candidates"
    n_ck = s // ck
    v_t = jnp.swapaxes(dv.reshape(b, n_ck, ck, dh), 2, 3)
    return pl.pallas_call(
        functools.partial(_dsa_kernel, tq=tq, ck=ck, topk=topk),
        grid=(b, s // tq),
        in_specs=[pl.BlockSpec((1, nh, tq, AUG), lambda bi, i: (bi, 0, i, 0)),
                  pl.BlockSpec((1, nh, tq, dh), lambda bi, i: (bi, 0, i, 0)),
                  pl.BlockSpec((1, 8, tq), lambda bi, i: (bi, 0, i)),
                  pl.BlockSpec((1, s, dh), lambda bi, i: (bi, 0, 0)),
                  pl.BlockSpec((1, s, AUG), lambda bi, i: (bi, 0, 0)),
                  pl.BlockSpec((1, n_ck, dh, ck), lambda bi, i: (bi, 0, 0, 0))],
        out_specs=pl.BlockSpec((1, nh * dh, tq), lambda bi, i: (bi, 0, i)),
        out_shape=jax.ShapeDtypeStruct((b, nh * dh, s), BF16),
        scratch_shapes=[pltpu.VMEM((n_ck, ck, tq), F32), pltpu.VMEM((n_ck, ck, tq), F32)],
        compiler_params=_cparams("parallel", "arbitrary"),
        name="dsa_attention",
    )(q_aug, iq, iw_t, ik, k_aug, v_t)


def _merge_kernel(h_ref, a_ref, b_ref, c_ref, d_ref, wg_ref, bg_ref, wb_ref, wo_ref, g_ref,
                  beta_ref, o_ref, *, alpha):
    h = h_ref[...]
    hb = h.astype(BF16)
    d = h.shape[1]
    acc = jnp.zeros(h.shape, F32)
    for n, br in enumerate((a_ref, b_ref, c_ref, d_ref)):
        gate = _sigmoid(_dot(hb, wg_ref[:, n * d:(n + 1) * d]) + bg_ref[:, n * d:(n + 1) * d])
        acc = acc + gate * _dot(br[...], wb_ref[n])
    mixed = _dot(acc.astype(BF16), wo_ref[...])
    o_ref[...] = _ln_rows(alpha * h + mixed, g_ref[...], beta_ref[...])


def _merge(h2d, branches, w_gate, b_gate, w_branch, w_out, ln_g, ln_b, alpha, tm):
    n, d = h2d.shape
    c = branches[0].shape[1]
    rows = lambda w: pl.BlockSpec((tm, w), lambda i: (i, 0))
    const = lambda shp: pl.BlockSpec(shp, lambda i: (0,) * len(shp))
    return pl.pallas_call(
        functools.partial(_merge_kernel, alpha=alpha),
        grid=(n // tm,),
        in_specs=[rows(d), rows(c), rows(c), rows(c), rows(c), const((d, N_BRANCH * d)),
                  const((1, N_BRANCH * d)), const((N_BRANCH, c, d)), const((d, d)),
                  const((1, d)), const((1, d))],
        out_specs=rows(d),
        out_shape=jax.ShapeDtypeStruct((n, d), F32),
        compiler_params=_cparams("parallel"),
        name="merge",
    )(h2d, *branches, w_gate, b_gate.reshape(1, -1), w_branch, w_out, ln_g.reshape(1, d),
      ln_b.reshape(1, d))


ROUTER_E0 = N_GROUPS
MOE_BLK = 256


def _routing_t(logits):
    row = lax.broadcasted_iota(I32, logits.shape, 0)
    far = jnp.int32(4 * LANES)
    neg = -jnp.inf

    def first_max(x):
        mx = jnp.max(x, axis=0, keepdims=True)
        idx = jnp.min(jnp.where(x == mx, row, far), axis=0, keepdims=True)
        return mx, idx

    lg = jnp.where(row < N_GROUPS, logits, neg)
    g_max, g_idx = first_max(lg)
    p_group = 1.0 / jnp.sum(jnp.exp(lg - g_max), axis=0, keepdims=True)
    lo = ROUTER_E0 + EXPERTS_PER_GROUP * g_idx
    le = jnp.where((row >= lo) & (row < lo + EXPERTS_PER_GROUP), logits, neg)
    m1, e1 = first_max(le)
    m2, e2 = first_max(jnp.where(row == e1, neg, le))
    r = jnp.exp(m2 - m1)
    gate1 = p_group * (1.0 / (1.0 + r))
    gate2 = p_group * (r / (1.0 + r))
    return jnp.where(row == e1, gate1, jnp.where(row == e2, gate2, 0.0)), g_idx


def _bf16_pieces(x, n):
    out, rest = [], x
    for _ in range(n):
        part = rest.astype(BF16)
        out.append(part)
        rest = rest - part.astype(F32)
    return out


def _moe_kernel(h_ref, wr_ref, br_ref, wg_ref, wu_ref, wd_ref, g_ref, beta_ref, o_ref, blk_ref, p_ref,
                xs_ref, gs_ref, acc_ref, *, alpha):
    e = pl.program_id(1)
    t, d = h_ref.shape
    f = wd_ref.shape[2]
    n_slots = p_ref.shape[0] // MOE_BLK

    @pl.when(e == 0)
    def _():
        h = h_ref[...]
        hb = h.astype(BF16)
        gates, g_idx = _routing_t(_dot_nt(wr_ref[...], hb) + br_ref[...])
        grp = lax.broadcasted_iota(I32, (SUBLANES, t), 0)
        onehot = jnp.where(grp == g_idx, 1.0, 0.0)
        r_i = lax.broadcasted_iota(I32, (t, t), 0)
        c_i = lax.broadcasted_iota(I32, (t, t), 1)
        before = jnp.where(r_i < c_i, 1.0, 0.0).astype(BF16)
        rank = _dot(onehot.astype(BF16), before)
        total = jnp.sum(onehot, axis=1, keepdims=True)
        start = jnp.int32(0)
        starts = jnp.zeros((SUBLANES, 1), F32)
        grp_col = lax.broadcasted_iota(I32, (SUBLANES, 1), 0)
        for g in range(N_GROUPS):
            n_g = (total[g, 0].astype(I32) + (MOE_BLK - 1)) // MOE_BLK
            blk_ref[2 * g] = start
            blk_ref[2 * g + 1] = n_g
            starts = jnp.where(grp_col == g, (start * MOE_BLK).astype(F32), starts)
            start = start + n_g
        blk_ref[2 * N_GROUPS] = start
        pos = jnp.sum(onehot * (starts + rank), axis=0, keepdims=True).astype(I32)
        slot = lax.broadcasted_iota(I32, (p_ref.shape[0], t), 0)
        p_ref[...] = jnp.where(slot == pos, 1.0, 0.0).astype(BF16)
        perm = p_ref[...]
        xs = _dot_nt(h.T.astype(BF16), perm).astype(BF16)
        gs = sum(_dot_nt(piece, perm) for piece in _bf16_pieces(gates, 3))
        for b in range(n_slots):
            cols = slice(b * MOE_BLK, (b + 1) * MOE_BLK)
            xs_ref[b] = xs[:, cols]
            acc_ref[b] = jnp.zeros((d, MOE_BLK), F32)
            for k in range(N_EXPERTS):
                gs_ref[b, k] = gs[ROUTER_E0 + k:ROUTER_E0 + k + 1, cols]

    grp_e = e // EXPERTS_PER_GROUP
    blk0 = blk_ref[2 * grp_e]

    def expert_block(i, _):
        b = blk0 + i
        xs = xs_ref[b]
        hg = _dot(wg_ref[0], xs)
        hu = _dot(wu_ref[0], xs)
        hid = (hg * _sigmoid(hg) * hu).astype(BF16)
        y = jnp.concatenate([_dot(wd_ref[0, :d // 2], hid), _dot(wd_ref[0, d // 2:], hid)], axis=0)
        acc_ref[b] += gs_ref[b, e] * y
        return 0

    lax.fori_loop(0, blk_ref[2 * grp_e + 1], expert_block, 0)

    @pl.when(e == pl.num_programs(1) - 1)
    def _():
        pieces = [jnp.concatenate(col, axis=1) for col in
                  zip(*[_bf16_pieces(acc_ref[b], 2) for b in range(n_slots)])]
        half = t // 2
        for c in range(2):
            cols = slice(c * half, (c + 1) * half)
            out_t = sum(_dot(piece, p_ref[:, cols]) for piece in pieces)
            o_ref[cols, :] = _ln_rows(alpha * h_ref[cols, :] + out_t.T, g_ref[...], beta_ref[...])


def _moe(h2d, w_router_t, b_router_t, wg_t, wu_t, wd_t, ln_g, ln_b, alpha, tm):
    n, d = h2d.shape
    ne, _, f = wd_t.shape
    assert tm % MOE_BLK == 0
    n_slots = tm // MOE_BLK + N_GROUPS
    rows = pl.BlockSpec((tm, d), lambda i, e: (i, 0))
    const = lambda shp: pl.BlockSpec(shp, lambda i, e: (0,) * len(shp))
    return pl.pallas_call(
        functools.partial(_moe_kernel, alpha=alpha),
        grid=(n // tm, ne),
        in_specs=[rows, const((LANES, d)), const((LANES, 1)),
                  pl.BlockSpec((1, f, d), lambda i, e: (e, 0, 0)),
                  pl.BlockSpec((1, f, d), lambda i, e: (e, 0, 0)),
                  pl.BlockSpec((1, d, f), lambda i, e: (e, 0, 0)),
                  const((1, d)), const((1, d))],
        out_specs=rows,
        out_shape=jax.ShapeDtypeStruct((n, d), F32),
        scratch_shapes=[pltpu.SMEM((2 * N_GROUPS + 1,), I32),
                        pltpu.VMEM((n_slots * MOE_BLK, tm), BF16),
                        pltpu.VMEM((n_slots, d, MOE_BLK), BF16),
                        pltpu.VMEM((n_slots, ne, 1, MOE_BLK), F32),
                        pltpu.VMEM((n_slots, d, MOE_BLK), F32)],
        compiler_params=pltpu.CompilerParams(dimension_semantics=("parallel", "arbitrary"),
                                             vmem_limit_bytes=MOE_VMEM_LIMIT),
        name="moe",
    )(h2d, w_router_t, b_router_t, wg_t, wu_t, wd_t, ln_g.reshape(1, d), ln_b.reshape(1, d))


def _tiles(b, s):
    n = b * s
    pick = lambda total, want: want if total % want == 0 else total
    return dict(
        ln_tm=pick(n, 512), proj_tm=pick(s, 512), local_ts=pick(s, 512),
        fox_tq=pick(s, 256), fox_ck=pick(s, 512), dsa_tq=pick(s, 256), dsa_ck=pick(s, 512),
        merge_tm=pick(n, 512), moe_tm=pick(n, 1024))


def kernel(x, ln_in_g, ln_in_b, w_in, b_forget, b_gate, pool_w, pool_scale, conv_dw, conv_b,
           conv_ln_g, conv_ln_b, w_branch, w_out, ln1_g, ln1_b, router_g, router_g_b, router_e,
           router_e_b, expert_w_gate, expert_w_up, expert_w_down, ln2_g, ln2_b):
    b, s, d = x.shape
    depth = w_in.shape[0]
    alpha = (2.0 * depth) ** 0.25
    t = _tiles(b, s)
    n = b * s

    h = _layer_norm(x.reshape(n, d), ln_in_g, ln_in_b, t["ln_tm"])
    for l in range(depth):
        wp = _pack_proj_weight(w_in[l])
        (pool_v, fq, fk, fv, dq, iq, glu, dk, dv, ik, small) = _projection(
            h.reshape(b, s, d), wp, t["proj_tm"])

        small_t = jnp.swapaxes(small[..., SMALL_FF:SMALL_FF + 2 * N_HEADS], 1, 2)
        f_t = jnp.pad(small_t[:, :N_HEADS], ((0, 0), (0, 8 - N_HEADS), (0, 0)))
        iw_t = jnp.pad(small_t[:, N_HEADS:], ((0, 0), (0, 8 - N_HEADS), (0, 0)))
        bf_col = jnp.pad(b_forget[l], (0, 8 - N_HEADS)).reshape(8, 1)
        c_parts = _forget_cumsum(f_t, bf_col)[:, :, :N_HEADS]

        pool_wbd = jnp.zeros((BRANCH_W, BRANCH_W), F32)
        for g in range(len(POOL_WINDOWS)):
            sl = slice(g * POOL_GROUP, (g + 1) * POOL_GROUP)
            pool_wbd = pool_wbd.at[sl, sl].set(pool_w[l, g])
        br_a, br_d = _local_mixers(pool_v, glu, pool_wbd.astype(BF16), pool_scale[l], conv_dw[l],
                                   conv_b[l], conv_ln_g[l], conv_ln_b[l], t["local_ts"])
        br_b = _fox_attention(fq, fk, fv, c_parts, t["fox_tq"], t["fox_ck"])
        br_c = _dsa_attention(dq, iq, iw_t, ik, dk, dv, t["dsa_tq"], t["dsa_ck"])
        br_b, br_c = jnp.swapaxes(br_b, 1, 2), jnp.swapaxes(br_c, 1, 2)

        branches = [br.reshape(n, BRANCH_W) for br in (br_a, br_b, br_c, br_d)]
        h = _merge(h, branches, w_in[l][:, GATE_OFF:].astype(BF16), b_gate[l],
                   w_branch[l].astype(BF16), w_out[l].astype(BF16), ln1_g[l], ln1_b[l], alpha,
                   t["merge_tm"])

        w_router_t = jnp.concatenate(
            [router_g[l], router_e[l], jnp.zeros((d, LANES - N_GROUPS - N_EXPERTS), F32)],
            axis=1).T.astype(BF16)
        b_router_t = jnp.concatenate(
            [router_g_b[l], router_e_b[l], jnp.zeros((LANES - N_GROUPS - N_EXPERTS,), F32)]
        ).reshape(LANES, 1)
        wg_t = jnp.swapaxes(expert_w_gate[l].astype(BF16), 1, 2)
        wu_t = jnp.swapaxes(expert_w_up[l].astype(BF16), 1, 2)
        wd_t = jnp.swapaxes(expert_w_down[l].astype(BF16), 1, 2)
        h = _moe(h, w_router_t, b_router_t, wg_t, wu_t, wd_t, ln2_g[l], ln2_b[l], alpha,
                 t["moe_tm"])
    return h.reshape(b, s, d)
```

```python
import functools

import jax
import jax.numpy as jnp
from jax import lax
from jax.experimental import pallas as pl
from jax.experimental.pallas import tpu as pltpu

F32 = jnp.float32
BF16 = jnp.bfloat16
I32 = jnp.int32

N_BRANCH = 4
HEAD_DIM = 64
N_HEADS = 4
BRANCH_W = N_HEADS * HEAD_DIM
POOL_WINDOWS = (2, 4, 8, 16)
POOL_GROUP = BRANCH_W // len(POOL_WINDOWS)
CONV_W = 31
DSA_TOPK = 256
N_GROUPS = 4
EXPERTS_PER_GROUP = 4
N_EXPERTS = N_GROUPS * EXPERTS_PER_GROUP
LN_EPS = 1e-5
NEG_INF = -1e30
ATTN_SCALE = HEAD_DIM ** -0.5

LANES = 128
HALO = 32
VMEM_LIMIT = 56 * 1024 * 1024
MOE_VMEM_LIMIT = 60 * 1024 * 1024

_OFF = {}
_o = 0
for _name, _n in (("pool", BRANCH_W), ("fq", BRANCH_W), ("fk", BRANCH_W), ("fv", BRANCH_W),
                  ("ff", N_HEADS), ("dq", BRANCH_W), ("dk", HEAD_DIM), ("dv", HEAD_DIM),
                  ("iq", BRANCH_W), ("ik", HEAD_DIM), ("iw", N_HEADS), ("conv", 2 * BRANCH_W)):
    _OFF[_name] = (_o, _o + _n)
    _o += _n
GATE_OFF = _o
SMALL_FF = HEAD_DIM
SMALL_IW = HEAD_DIM + N_HEADS
AUG = 2 * HEAD_DIM
POS_DIGIT_BITS = 6


def _cparams(*sem):
    return pltpu.CompilerParams(dimension_semantics=sem, vmem_limit_bytes=VMEM_LIMIT)


def _ln_rows(x, g, b):
    mu = jnp.mean(x, axis=-1, keepdims=True)
    xc = x - mu
    var = jnp.mean(xc * xc, axis=-1, keepdims=True)
    return xc * lax.rsqrt(var + LN_EPS) * g + b


def _sigmoid(x):
    return 1.0 / (1.0 + jnp.exp(-x))


def _dot(a, b):
    return jnp.dot(a, b, preferred_element_type=F32)


def _dot_nt(a, b):
    return lax.dot_general(a, b, (((1,), (1,)), ((), ())), preferred_element_type=F32)


def _ln_kernel(x_ref, g_ref, b_ref, o_ref):
    o_ref[...] = _ln_rows(x_ref[...], g_ref[...], b_ref[...])


def _layer_norm(x2d, g, b, tm):
    n, d = x2d.shape
    return pl.pallas_call(
        _ln_kernel,
        grid=(n // tm,),
        in_specs=[pl.BlockSpec((tm, d), lambda i: (i, 0)),
                  pl.BlockSpec((1, d), lambda i: (0, 0)),
                  pl.BlockSpec((1, d), lambda i: (0, 0))],
        out_specs=pl.BlockSpec((tm, d), lambda i: (i, 0)),
        out_shape=jax.ShapeDtypeStruct((n, d), F32),
        compiler_params=_cparams("parallel"),
        name="ln_in",
    )(x2d, g.reshape(1, d), b.reshape(1, d))


_PK = {"pool": 0, "fq": 256, "fk": 512, "fv": 768, "dq": 1024, "iq": 1280, "conv": 1536,
       "dkv": 2048, "small": 2176}
PK_COLS = 2304


def _pack_proj_weight(w):
    d = w.shape[0]
    sl = lambda k: w[:, _OFF[k][0]:_OFF[k][1]]
    small = jnp.concatenate([sl("ik"), sl("ff"), sl("iw"),
                             jnp.zeros((d, LANES - HEAD_DIM - 2 * N_HEADS), w.dtype)], axis=1)
    packed = jnp.concatenate([sl("pool"), sl("fq") * ATTN_SCALE, sl("fk"), sl("fv"),
                              sl("dq") * ATTN_SCALE, sl("iq"), sl("conv"), sl("dk"), sl("dv"),
                              small], axis=1)
    return packed.astype(BF16)


def _proj_kernel(h_ref, w_ref, pool_o, fq_o, fk_o, fv_o, dq_o, iq_o, glu_o, dk_o, dv_o, ik_o,
                 small_o):
    hb = h_ref[0].astype(BF16)
    tm = hb.shape[0]

    def piece(name, width):
        return _dot(hb, w_ref[:, _PK[name]:_PK[name] + width])

    lane = lax.broadcasted_iota(I32, (tm, HEAD_DIM), 1)
    pos = pl.program_id(1) * tm + lax.broadcasted_iota(I32, (tm, HEAD_DIM), 0)
    p_hi = (pos >> POS_DIGIT_BITS).astype(F32)
    p_lo = (pos & ((1 << POS_DIGIT_BITS) - 1)).astype(F32)
    digit = float(1 << POS_DIGIT_BITS)

    def bias_cols(c0, c1, c2, c3):
        cols = jnp.where(lane == 0, c0, jnp.where(lane == 1, c1, jnp.where(
            lane == 2, c2, jnp.where(lane == 3, c3, 0.0))))
        return cols.astype(BF16)

    def widen(r, cols):
        return jnp.concatenate([r, cols], axis=1)

    pool_o[0] = piece("pool", BRANCH_W)
    for name, out in (("fq", fq_o), ("fk", fk_o), ("fv", fv_o), ("dq", dq_o), ("iq", iq_o)):
        r = piece(name, BRANCH_W).astype(BF16)
        for h in range(N_HEADS):
            r_h = r[:, h * HEAD_DIM:(h + 1) * HEAD_DIM]
            if name == "fq":
                r_h = widen(r_h, bias_cols(-1.0, -1.0, -1.0, 0.0))
            elif name == "dq":
                slope = 2.0 ** (-8.0 * (h + 1) / N_HEADS)
                r_h = widen(r_h, bias_cols(digit * slope, slope, -digit * slope * p_hi,
                                           -slope * p_lo))
            out[0, h] = r_h
    a = piece("conv", 2 * BRANCH_W)
    glu_o[0] = a[:, :BRANCH_W] * _sigmoid(a[:, BRANCH_W:])
    kv = piece("dkv", 2 * HEAD_DIM).astype(BF16)
    dk_o[0] = widen(kv[:, :HEAD_DIM], bias_cols(p_hi, p_lo, 1.0, 1.0))
    dv_o[0] = kv[:, HEAD_DIM:]
    sm = piece("small", LANES)
    small_o[0] = sm
    ik_o[0] = sm[:, :HEAD_DIM].astype(BF16)


def _projection(h, wp, tm):
    b, s, d = h.shape
    assert s <= (1 << POS_DIGIT_BITS) ** 2, "positions are carried as two digits"
    heads = lambda w: pl.BlockSpec((1, N_HEADS, tm, w), lambda bi, i: (bi, 0, i, 0))
    rows = lambda w: pl.BlockSpec((1, tm, w), lambda bi, i: (bi, i, 0))
    head_shape = lambda w: jax.ShapeDtypeStruct((b, N_HEADS, s, w), BF16)
    hd, aug = HEAD_DIM, AUG
    return pl.pallas_call(
        _proj_kernel,
        grid=(b, s // tm),
        in_specs=[rows(d), pl.BlockSpec((d, PK_COLS), lambda bi, i: (0, 0))],
        out_specs=[rows(BRANCH_W), heads(aug), heads(hd), heads(hd), heads(aug), heads(hd),
                   rows(BRANCH_W), rows(aug), rows(hd), rows(hd), rows(LANES)],
        out_shape=[jax.ShapeDtypeStruct((b, s, BRANCH_W), F32), head_shape(aug), head_shape(hd),
                   head_shape(hd), head_shape(aug), head_shape(hd),
                   jax.ShapeDtypeStruct((b, s, BRANCH_W), F32),
                   jax.ShapeDtypeStruct((b, s, AUG), BF16),
                   jax.ShapeDtypeStruct((b, s, HEAD_DIM), BF16),
                   jax.ShapeDtypeStruct((b, s, HEAD_DIM), BF16),
                   jax.ShapeDtypeStruct((b, s, LANES), F32)],
        compiler_params=_cparams("parallel", "parallel"),
        name="proj",
    )(h, wp)


def _forget_cumsum_kernel(f_ref, b_ref, o_ref):
    s = f_ref.shape[2]
    x = f_ref[0] + b_ref[...]
    ls = jnp.minimum(x, 0.0) - jnp.log(1.0 + jnp.exp(-jnp.abs(x)))
    r = lax.broadcasted_iota(I32, (LANES, LANES), 0)
    c = lax.broadcasted_iota(I32, (LANES, LANES), 1)
    upper = jnp.where(r <= c, 1.0, 0.0).astype(F32)
    carry = jnp.zeros((x.shape[0], 1), F32)
    for j in range(s // LANES):
        blk = jnp.dot(ls[:, j * LANES:(j + 1) * LANES], upper, preferred_element_type=F32,
                      precision=lax.Precision.HIGHEST) + carry
        carry = blk[:, LANES - 1:LANES]
        rest = blk
        for piece in range(3):
            part = rest.astype(BF16)
            o_ref[0, piece, :, j * LANES:(j + 1) * LANES] = part
            rest = rest - part.astype(F32)


def _forget_cumsum(f_t, b_col):
    b, r, s = f_t.shape
    return pl.pallas_call(
        _forget_cumsum_kernel,
        grid=(b,),
        in_specs=[pl.BlockSpec((1, r, s), lambda bi: (bi, 0, 0)),
                  pl.BlockSpec((r, 1), lambda bi: (0, 0))],
        out_specs=pl.BlockSpec((1, 3, r, s), lambda bi: (bi, 0, 0, 0)),
        out_shape=jax.ShapeDtypeStruct((b, 3, r, s), BF16),
        compiler_params=_cparams("parallel"),
        name="forget_cumsum",
    )(f_t, b_col)


def _local_kernel(pv_ref, pvh_ref, gl_ref, glh_ref, pw_ref, ps_ref, dw_ref, cb_ref, g_ref, b_ref,
                  a_o, d_o, pbuf, cbuf, shifted, *, ts):
    i = pl.program_id(1)
    first = i == 0
    pbuf[0:HALO, :] = jnp.where(first, 0.0, pvh_ref[0])
    pbuf[HALO:HALO + ts, :] = pv_ref[0]
    cbuf[0:HALO, :] = jnp.where(first, 0.0, glh_ref[0])
    cbuf[HALO:HALO + ts, :] = gl_ref[0]

    def shifted_window(buf, back):
        whole, r = divmod(back, SUBLANES)
        lo = HALO - whole * SUBLANES
        return buf[lo:lo + ts, :] if r == 0 else shifted[r - 1, lo:lo + ts, :]

    def fill_shifted(buf):
        for r in range(1, SUBLANES):
            shifted[r - 1, SUBLANES:, :] = buf[SUBLANES - r:HALO + ts - r, :]

    fill_shifted(pbuf)
    lane = lax.broadcasted_iota(I32, (ts, BRANCH_W), 1)
    pos = (i * ts + 1 + lax.broadcasted_iota(I32, (ts, 1), 0)).astype(F32)
    v = pbuf[HALO:HALO + ts, :]
    run = v
    pooled = jnp.zeros((ts, BRANCH_W), F32)
    shift = 1
    for g, w in enumerate(POOL_WINDOWS):
        while shift < w:
            run = run + shifted_window(pbuf, shift)
            shift += 1
        in_group = (lane >= g * POOL_GROUP) & (lane < (g + 1) * POOL_GROUP)
        pooled = jnp.where(in_group, run / jnp.minimum(pos, float(w)), pooled)
    pooled = pooled - v
    a_o[0] = (_dot(pooled.astype(BF16), pw_ref[...]) * ps_ref[...]).astype(BF16)

    fill_shifted(cbuf)
    y = jnp.zeros((ts, BRANCH_W), F32) + cb_ref[...]
    for j in range(CONV_W):
        y = y + shifted_window(cbuf, CONV_W - 1 - j) * dw_ref[j:j + 1, :]
    z = _ln_rows(y, g_ref[...], b_ref[...])
    d_o[0] = (z * _sigmoid(z)).astype(BF16)


def _local_mixers(pool_v, glu, pool_wbd, pool_scale, conv_dw, conv_b, ln_g, ln_b, ts):
    b, s, c = pool_v.shape
    cur = pl.BlockSpec((1, ts, c), lambda bi, i: (bi, i, 0))
    halo = pl.BlockSpec((1, HALO, c), lambda bi, i: (bi, jnp.maximum(i * (ts // HALO) - 1, 0), 0))
    full2 = lambda shp: pl.BlockSpec(shp, lambda bi, i: (0, 0))
    out_shape = jax.ShapeDtypeStruct((b, s, c), BF16)
    return pl.pallas_call(
        functools.partial(_local_kernel, ts=ts),
        grid=(b, s // ts),
        in_specs=[cur, halo, cur, halo, full2((c, c)), full2((1, c)), full2((CONV_W, c)),
                  full2((1, c)), full2((1, c)), full2((1, c))],
        out_specs=[cur, cur],
        out_shape=[out_shape, out_shape],
        scratch_shapes=[pltpu.VMEM((HALO + ts, c), F32), pltpu.VMEM((HALO + ts, c), F32),
                        pltpu.VMEM((SUBLANES - 1, HALO + ts, c), F32)],
        compiler_params=_cparams("parallel", "parallel"),
        name="local_mixers",
    )(pool_v, pool_v, glu, glu, pool_wbd, pool_scale.reshape(1, c), conv_dw, conv_b.reshape(1, c),
      ln_g.reshape(1, c), ln_b.reshape(1, c))


SUBLANES = 8
FOLD_CHAINS = 8


def _fold_rows(x, op):
    slabs = [x[i:i + SUBLANES] for i in range(0, x.shape[0], SUBLANES)]
    accs = slabs[:FOLD_CHAINS]
    for i, slab in enumerate(slabs[FOLD_CHAINS:]):
        accs[i % FOLD_CHAINS] = op(accs[i % FOLD_CHAINS], slab)
    while len(accs) > 1:
        accs = [op(accs[i], accs[i + 1]) for i in range(0, len(accs) - 1, 2)] + (
            [accs[-1]] if len(accs) % 2 else [])
    return accs[0]


def _softmax_chunk(s, m, l):
    m_new = jnp.maximum(m, jnp.max(_fold_rows(s, jnp.maximum), axis=0, keepdims=True))
    alpha = jnp.exp(m - m_new)
    p = jnp.exp(s - m_new)
    l_new = alpha * l + jnp.sum(_fold_rows(p, jnp.add), axis=0, keepdims=True)
    return m_new, l_new, alpha, p.astype(BF16)


def _causal_split(qi, tq, ck):
    return (qi * tq + 1) // ck, ((qi + 1) * tq + ck - 1) // ck


CHUNK_STEPS = (4, 2, 1)


def _attention_loops(n_plain, n_chunks, tq, scores, adjust, edit_last, values):
    def consume(j, ss, carry, edit):
        stats, ps = [], []
        for h in range(N_HEADS):
            s = adjust(j, ss[h])
            if edit:
                s = edit_last(j, s)
            m_new, l_new, alpha, p = _softmax_chunk(s, carry[h][0], carry[h][1])
            stats.append((m_new, l_new, alpha))
            ps.append(p)
        out = []
        for h in range(N_HEADS):
            m_new, l_new, alpha = stats[h]
            out.append((m_new, l_new, alpha * carry[h][2] + _dot(values(j, h), ps[h])))
        return out

    def step(j0, carry, n, edit):
        ss = [scores(j0 + c) for c in range(n)]
        carry = list(carry)
        for c in range(n):
            carry = consume(j0 + c, ss[c], carry, edit)
        return tuple(carry)

    carry = tuple((jnp.full((1, tq), NEG_INF, F32), jnp.zeros((1, tq), F32),
                   jnp.zeros((HEAD_DIM, tq), F32)) for _ in range(N_HEADS))
    done = 0
    for width in CHUNK_STEPS:
        n_steps = (n_plain - done) // width
        carry = lax.fori_loop(
            0, n_steps, lambda i, c, w=width, d0=done: step(d0 + i * w, c, w, False), carry)
        done = done + n_steps * width
    if edit_last is None:
        return carry
    return lax.fori_loop(n_plain, n_chunks, lambda j, c: step(j, c, 1, True), carry)


def _fox_kernel(q_ref, k_ref, vt_ref, o_ref, *, tq, ck):
    qi = pl.program_id(1)
    n_full, n_chunks = _causal_split(qi, tq, ck)
    qpos = qi * tq + lax.broadcasted_iota(I32, (1, tq), 1)
    kiota = lax.broadcasted_iota(I32, (ck, 1), 0)
    qs = [q_ref[0, h] for h in range(N_HEADS)]

    def scores(j):
        start = pl.multiple_of(j * ck, ck)
        return [_dot_nt(k_ref[0, h, pl.ds(start, ck), :], qs[h]) for h in range(N_HEADS)]

    def causal(j, s):
        return jnp.where(j * ck + kiota <= qpos, s, NEG_INF)

    def values(j, h):
        return vt_ref[0, h, j]

    carry = _attention_loops(n_full, n_chunks, tq, scores, lambda j, s: s, causal, values)
    for h in range(N_HEADS):
        _, l, acc = carry[h]
        o_ref[0, h * HEAD_DIM:(h + 1) * HEAD_DIM, :] = (acc / l).astype(BF16)


def _fox_attention(q_aug, fk, fv, c_parts, tq, ck):
    b, nh, s, dh = fk.shape
    n_ck = s // ck
    pad = jnp.zeros((b, nh, s, AUG - dh - 3), BF16)
    k_aug = jnp.concatenate([fk, jnp.moveaxis(c_parts, 1, 3), pad], axis=-1)
    v_t = jnp.swapaxes(fv.reshape(b, nh, n_ck, ck, dh), 3, 4)
    return pl.pallas_call(
        functools.partial(_fox_kernel, tq=tq, ck=ck),
        grid=(b, s // tq),
        in_specs=[pl.BlockSpec((1, nh, tq, AUG), lambda bi, i: (bi, 0, i, 0)),
                  pl.BlockSpec((1, nh, s, AUG), lambda bi, i: (bi, 0, 0, 0)),
                  pl.BlockSpec((1, nh, n_ck, dh, ck), lambda bi, i: (bi, 0, 0, 0, 0))],
        out_specs=pl.BlockSpec((1, nh * dh, tq), lambda bi, i: (bi, 0, i)),
        out_shape=jax.ShapeDtypeStruct((b, nh * dh, s), BF16),
        compiler_params=_cparams("parallel", "arbitrary"),
        name="fox_attention",
    )(q_aug, k_aug, v_t)


def _key_to_f32(key):
    bits = key ^ (lax.shift_right_arithmetic(key, 31) & 0x7FFFFFFF)
    return lax.bitcast_convert_type(bits, F32)


def _f32_to_key(x):
    bits = lax.bitcast_convert_type(x, I32)
    return bits ^ (lax.shift_right_arithmetic(bits, 31) & 0x7FFFFFFF)


KEY_POS_INF = 0x7F800000
KEY_MIN_NORMAL = 0x00800000
KEY_NEG_INF = 0x007FFFFF - 2 ** 31
PROBE_STEP = 1 << 24
MAX_PROBES = 36
PROBES_PER_CHECK = 3


def _dsa_kernel(dq_ref, iq_ref, iwt_ref, ik_ref, dk_ref, dvt_ref, o_ref, score_buf, bias_buf, *,
                tq, ck, topk):
    qi = pl.program_id(1)
    n_full, n_chunks = _causal_split(qi, tq, ck)
    qpos = qi * tq + lax.broadcasted_iota(I32, (1, tq), 1)
    kiota = lax.broadcasted_iota(I32, (ck, 1), 0)
    w_idx = iwt_ref[0]
    kf = float(topk)

    def count(pred):
        def cb(j, acc):
            return acc + _fold_rows(jnp.where(pred(score_buf[j]), 1.0, 0.0), jnp.add)
        acc = lax.fori_loop(0, n_chunks, cb, jnp.zeros((SUBLANES, tq), F32))
        return jnp.sum(acc, axis=0, keepdims=True)

    def score_body(j, stats, masked):
        top, n_pos, n_nn = stats
        start = pl.multiple_of(j * ck, ck)
        ikc = ik_ref[0, pl.ds(start, ck), :]
        sc = None
        for h in range(N_HEADS):
            term = jnp.maximum(_dot_nt(ikc, iq_ref[0, h]), 0.0) * w_idx[h:h + 1, :]
            sc = term if sc is None else sc + term
        if masked:
            sc = jnp.where(start + kiota <= qpos, sc, NEG_INF)
        score_buf[j] = sc
        return (jnp.maximum(top, _fold_rows(sc, jnp.maximum)),
                n_pos + _fold_rows(jnp.where(sc > 0.0, 1.0, 0.0), jnp.add),
                n_nn + _fold_rows(jnp.where(sc >= 0.0, 1.0, 0.0), jnp.add))

    stats = (jnp.full((SUBLANES, tq), -jnp.inf, F32), jnp.zeros((SUBLANES, tq), F32),
             jnp.zeros((SUBLANES, tq), F32))
    stats = lax.fori_loop(0, n_full, functools.partial(score_body, masked=False), stats)
    stats = lax.fori_loop(n_full, n_chunks, functools.partial(score_body, masked=True), stats)

    top_key = _f32_to_key(jnp.max(stats[0], axis=0, keepdims=True))
    c_pos = jnp.sum(stats[1], axis=0, keepdims=True)
    c_nn = jnp.sum(stats[2], axis=0, keepdims=True)
    pos_side, zero_tie = c_pos >= kf, c_nn >= kf
    n_tot = (n_chunks * ck).astype(F32)
    lo = jnp.where(pos_side, KEY_MIN_NORMAL, jnp.where(zero_tie, 0, KEY_NEG_INF)).astype(I32)
    hi = jnp.where(pos_side, KEY_POS_INF, jnp.where(zero_tie, 1, 0)).astype(I32)
    c_lo = jnp.where(pos_side, c_pos, jnp.where(zero_tie, c_nn, n_tot))
    c_hi = jnp.where(pos_side, 0.0, jnp.where(zero_tie, c_pos, c_nn))

    def unsettled(lo, hi, c_lo):
        return (c_lo != kf) & (hi > lo + 1)

    def probe(state):
        it, lo, hi, c_lo, c_hi = state
        mid = (lo >> 1) + (hi >> 1) + (lo & hi & 1)
        near_top = jnp.where(hi > 0, jnp.maximum(mid, hi - PROBE_STEP), mid)
        cand = jnp.where(it == 0, top_key, jnp.where(it == 1, near_top, mid))
        cand = jnp.minimum(jnp.maximum(cand, lo + 1), hi - 1)
        cand_f = _key_to_f32(cand)
        c = count(lambda s: s >= cand_f)
        live = unsettled(lo, hi, c_lo)
        up = live & (c >= kf)
        down = live & (c < kf)
        return (it + 1, jnp.where(up, cand, lo), jnp.where(down, cand, hi),
                jnp.where(up, c, c_lo), jnp.where(down, c, c_hi))

    def searching(state):
        it, lo, hi, c_lo, _ = state
        left = jnp.max(jnp.where(unsettled(lo, hi, c_lo), 1.0, 0.0))
        return (it < MAX_PROBES) & (left > 0.0)

    def probes(state):
        for _ in range(PROBES_PER_CHECK):
            state = probe(state)
        return state

    _, lo, hi, _, c_hi = lax.while_loop(searching, probes, (jnp.int32(0), lo, hi, c_lo, c_hi))
    thr = _key_to_f32(lo)

    need = jnp.where(hi == lo + 1, kf - c_hi, float(ck) * n_tot)
    r_i = lax.broadcasted_iota(I32, (ck, ck), 0)
    c_i = lax.broadcasted_iota(I32, (ck, ck), 1)
    lower = jnp.where(c_i <= r_i, 1.0, 0.0).astype(BF16)

    def bias_body(j, seen, masked):
        start = pl.multiple_of(j * ck, ck)
        sc = score_buf[j]
        tie = sc == thr
        rank = _dot(lower, jnp.where(tie, 1.0, 0.0).astype(BF16)) + seen
        keep = jnp.where(sc > thr, 0.0, jnp.where(tie, jnp.where(rank <= need, 0.0, NEG_INF),
                                                  NEG_INF))
        if masked:
            keep = jnp.where(start + kiota <= qpos, keep, NEG_INF)
        bias_buf[j] = keep
        return rank[ck - 1:ck, :]

    seen = lax.fori_loop(0, n_full, functools.partial(bias_body, masked=False),
                         jnp.zeros((1, tq), F32))
    lax.fori_loop(n_full, n_chunks, functools.partial(bias_body, masked=True), seen)

    qs = [dq_ref[0, h] for h in range(N_HEADS)]

    def scores(j):
        start = pl.multiple_of(j * ck, ck)
        kc = dk_ref[0, pl.ds(start, ck), :]
        bias = bias_buf[j]
        return [_dot_nt(kc, qs[h]) + bias for h in range(N_HEADS)]

    carry = _attention_loops(n_chunks, n_chunks, tq, scores, lambda j, s: s, None,
                             lambda j, h: dvt_ref[0, j])
    for h in range(N_HEADS):
        _, l, acc = carry[h]
        o_ref[0, h * HEAD_DIM:(h + 1) * HEAD_DIM, :] = (acc / l).astype(BF16)


def _dsa_attention(q_aug, iq, iw_t, ik, k_aug, dv, tq, ck):
    b, nh, s, dh = iq.shape
    topk = min(DSA_TOPK, s // 4)
    assert ck >= topk, "every query must see at least topk (possibly masked) candidates"
    n_ck = s // ck
    v_t = jnp.swapaxes(dv.reshape(b, n_ck, ck, dh), 2, 3)
    return pl.pallas_call(
        functools.partial(_dsa_kernel, tq=tq, ck=ck, topk=topk),
        grid=(b, s // tq),
        in_specs=[pl.BlockSpec((1, nh, tq, AUG), lambda bi, i: (bi, 0, i, 0)),
                  pl.BlockSpec((1, nh, tq, dh), lambda bi, i: (bi, 0, i, 0)),
                  pl.BlockSpec((1, 8, tq), lambda bi, i: (bi, 0, i)),
                  pl.BlockSpec((1, s, dh), lambda bi, i: (bi, 0, 0)),
                  pl.BlockSpec((1, s, AUG), lambda bi, i: (bi, 0, 0)),
                  pl.BlockSpec((1, n_ck, dh, ck), lambda bi, i: (bi, 0, 0, 0))],
        out_specs=pl.BlockSpec((1, nh * dh, tq), lambda bi, i: (bi, 0, i)),
        out_shape=jax.ShapeDtypeStruct((b, nh * dh, s), BF16),
        scratch_shapes=[pltpu.VMEM((n_ck, ck, tq), F32), pltpu.VMEM((n_ck, ck, tq), F32)],
        compiler_params=_cparams("parallel", "arbitrary"),
        name="dsa_attention",
    )(q_aug, iq, iw_t, ik, k_aug, v_t)


def _merge_kernel(h_ref, a_ref, b_ref, c_ref, d_ref, wg_ref, bg_ref, wb_ref, wo_ref, g_ref,
                  beta_ref, o_ref, *, alpha):
    h = h_ref[...]
    hb = h.astype(BF16)
    d = h.shape[1]
    acc = jnp.zeros(h.shape, F32)
    for n, br in enumerate((a_ref, b_ref, c_ref, d_ref)):
        gate = _sigmoid(_dot(hb, wg_ref[:, n * d:(n + 1) * d]) + bg_ref[:, n * d:(n + 1) * d])
        acc = acc + gate * _dot(br[...], wb_ref[n])
    mixed = _dot(acc.astype(BF16), wo_ref[...])
    o_ref[...] = _ln_rows(alpha * h + mixed, g_ref[...], beta_ref[...])


def _merge(h2d, branches, w_gate, b_gate, w_branch, w_out, ln_g, ln_b, alpha, tm):
    n, d = h2d.shape
    c = branches[0].shape[1]
    rows = lambda w: pl.BlockSpec((tm, w), lambda i: (i, 0))
    const = lambda shp: pl.BlockSpec(shp, lambda i: (0,) * len(shp))
    return pl.pallas_call(
        functools.partial(_merge_kernel, alpha=alpha),
        grid=(n // tm,),
        in_specs=[rows(d), rows(c), rows(c), rows(c), rows(c), const((d, N_BRANCH * d)),
                  const((1, N_BRANCH * d)), const((N_BRANCH, c, d)), const((d, d)),
                  const((1, d)), const((1, d))],
        out_specs=rows(d),
        out_shape=jax.ShapeDtypeStruct((n, d), F32),
        compiler_params=_cparams("parallel"),
        name="merge",
    )(h2d, *branches, w_gate, b_gate.reshape(1, -1), w_branch, w_out, ln_g.reshape(1, d),
      ln_b.reshape(1, d))


ROUTER_E0 = N_GROUPS
MOE_BLK = 256


def _routing_t(logits):
    row = lax.broadcasted_iota(I32, logits.shape, 0)
    far = jnp.int32(4 * LANES)
    neg = -jnp.inf

    def first_max(x):
        mx = jnp.max(x, axis=0, keepdims=True)
        idx = jnp.min(jnp.where(x == mx, row, far), axis=0, keepdims=True)
        return mx, idx

    lg = jnp.where(row < N_GROUPS, logits, neg)
    g_max, g_idx = first_max(lg)
    p_group = 1.0 / jnp.sum(jnp.exp(lg - g_max), axis=0, keepdims=True)
    lo = ROUTER_E0 + EXPERTS_PER_GROUP * g_idx
    le = jnp.where((row >= lo) & (row < lo + EXPERTS_PER_GROUP), logits, neg)
    m1, e1 = first_max(le)
    m2, e2 = first_max(jnp.where(row == e1, neg, le))
    r = jnp.exp(m2 - m1)
    gate1 = p_group * (1.0 / (1.0 + r))
    gate2 = p_group * (r / (1.0 + r))
    return jnp.where(row == e1, gate1, jnp.where(row == e2, gate2, 0.0)), g_idx


def _bf16_pieces(x, n):
    out, rest = [], x
    for _ in range(n):
        part = rest.astype(BF16)
        out.append(part)
        rest = rest - part.astype(F32)
    return out


def _moe_kernel(h_ref, wr_ref, br_ref, wg_ref, wu_ref, wd_ref, g_ref, beta_ref, o_ref, blk_ref, p_ref,
                xs_ref, gs_ref, acc_ref, *, alpha):
    e = pl.program_id(1)
    t, d = h_ref.shape
    f = wd_ref.shape[2]
    n_slots = p_ref.shape[0] // MOE_BLK

    @pl.when(e == 0)
    def _():
        h = h_ref[...]
        hb = h.astype(BF16)
        gates, g_idx = _routing_t(_dot_nt(wr_ref[...], hb) + br_ref[...])
        grp = lax.broadcasted_iota(I32, (SUBLANES, t), 0)
        onehot = jnp.where(grp == g_idx, 1.0, 0.0)
        r_i = lax.broadcasted_iota(I32, (t, t), 0)
        c_i = lax.broadcasted_iota(I32, (t, t), 1)
        before = jnp.where(r_i < c_i, 1.0, 0.0).astype(BF16)
        rank = _dot(onehot.astype(BF16), before)
        total = jnp.sum(onehot, axis=1, keepdims=True)
        start = jnp.int32(0)
        starts = jnp.zeros((SUBLANES, 1), F32)
        grp_col = lax.broadcasted_iota(I32, (SUBLANES, 1), 0)
        for g in range(N_GROUPS):
            n_g = (total[g, 0].astype(I32) + (MOE_BLK - 1)) // MOE_BLK
            blk_ref[2 * g] = start
            blk_ref[2 * g + 1] = n_g
            starts = jnp.where(grp_col == g, (start * MOE_BLK).astype(F32), starts)
            start = start + n_g
        blk_ref[2 * N_GROUPS] = start
        pos = jnp.sum(onehot * (starts + rank), axis=0, keepdims=True).astype(I32)
        slot = lax.broadcasted_iota(I32, (p_ref.shape[0], t), 0)
        p_ref[...] = jnp.where(slot == pos, 1.0, 0.0).astype(BF16)
        perm = p_ref[...]
        xs = _dot_nt(h.T.astype(BF16), perm).astype(BF16)
        gs = sum(_dot_nt(piece, perm) for piece in _bf16_pieces(gates, 3))
        for b in range(n_slots):
            cols = slice(b * MOE_BLK, (b + 1) * MOE_BLK)
            xs_ref[b] = xs[:, cols]
            acc_ref[b] = jnp.zeros((d, MOE_BLK), F32)
            for k in range(N_EXPERTS):
                gs_ref[b, k] = gs[ROUTER_E0 + k:ROUTER_E0 + k + 1, cols]

    grp_e = e // EXPERTS_PER_GROUP
    blk0 = blk_ref[2 * grp_e]

    def expert_block(i, _):
        b = blk0 + i
        xs = xs_ref[b]
        hg = _dot(wg_ref[0], xs)
        hu = _dot(wu_ref[0], xs)
        hid = (hg * _sigmoid(hg) * hu).astype(BF16)
        y = jnp.concatenate([_dot(wd_ref[0, :d // 2], hid), _dot(wd_ref[0, d // 2:], hid)], axis=0)
        acc_ref[b] += gs_ref[b, e] * y
        return 0

    lax.fori_loop(0, blk_ref[2 * grp_e + 1], expert_block, 0)

    @pl.when(e == pl.num_programs(1) - 1)
    def _():
        pieces = [jnp.concatenate(col, axis=1) for col in
                  zip(*[_bf16_pieces(acc_ref[b], 2) for b in range(n_slots)])]
        half = t // 2
        for c in range(2):
            cols = slice(c * half, (c + 1) * half)
            out_t = sum(_dot(piece, p_ref[:, cols]) for piece in pieces)
            o_ref[cols, :] = _ln_rows(alpha * h_ref[cols, :] + out_t.T, g_ref[...], beta_ref[...])


def _moe(h2d, w_router_t, b_router_t, wg_t, wu_t, wd_t, ln_g, ln_b, alpha, tm):
    n, d = h2d.shape
    ne, _, f = wd_t.shape
    assert tm % MOE_BLK == 0
    n_slots = tm // MOE_BLK + N_GROUPS
    rows = pl.BlockSpec((tm, d), lambda i, e: (i, 0))
    const = lambda shp: pl.BlockSpec(shp, lambda i, e: (0,) * len(shp))
    return pl.pallas_call(
        functools.partial(_moe_kernel, alpha=alpha),
        grid=(n // tm, ne),
        in_specs=[rows, const((LANES, d)), const((LANES, 1)),
                  pl.BlockSpec((1, f, d), lambda i, e: (e, 0, 0)),
                  pl.BlockSpec((1, f, d), lambda i, e: (e, 0, 0)),
                  pl.BlockSpec((1, d, f), lambda i, e: (e, 0, 0)),
                  const((1, d)), const((1, d))],
        out_specs=rows,
        out_shape=jax.ShapeDtypeStruct((n, d), F32),
        scratch_shapes=[pltpu.SMEM((2 * N_GROUPS + 1,), I32),
                        pltpu.VMEM((n_slots * MOE_BLK, tm), BF16),
                        pltpu.VMEM((n_slots, d, MOE_BLK), BF16),
                        pltpu.VMEM((n_slots, ne, 1, MOE_BLK), F32),
                        pltpu.VMEM((n_slots, d, MOE_BLK), F32)],
        compiler_params=pltpu.CompilerParams(dimension_semantics=("parallel", "arbitrary"),
                                             vmem_limit_bytes=MOE_VMEM_LIMIT),
        name="moe",
    )(h2d, w_router_t, b_router_t, wg_t, wu_t, wd_t, ln_g.reshape(1, d), ln_b.reshape(1, d))


def _tiles(b, s):
    n = b * s
    pick = lambda total, want: want if total % want == 0 else total
    return dict(
        ln_tm=pick(n, 512), proj_tm=pick(s, 512), local_ts=pick(s, 512),
        fox_tq=pick(s, 256), fox_ck=pick(s, 512), dsa_tq=pick(s, 256), dsa_ck=pick(s, 512),
        merge_tm=pick(n, 512), moe_tm=pick(n, 1024))


def kernel(x, ln_in_g, ln_in_b, w_in, b_forget, b_gate, pool_w, pool_scale, conv_dw, conv_b,
           conv_ln_g, conv_ln_b, w_branch, w_out, ln1_g, ln1_b, router_g, router_g_b, router_e,
           router_e_b, expert_w_gate, expert_w_up, expert_w_down, ln2_g, ln2_b):
    b, s, d = x.shape
    depth = w_in.shape[0]
    alpha = (2.0 * depth) ** 0.25
    t = _tiles(b, s)
    n = b * s

    h = _layer_norm(x.reshape(n, d), ln_in_g, ln_in_b, t["ln_tm"])
    for l in range(depth):
        wp = _pack_proj_weight(w_in[l])
        (pool_v, fq, fk, fv, dq, iq, glu, dk, dv, ik, small) = _projection(
            h.reshape(b, s, d), wp, t["proj_tm"])

        small_t = jnp.swapaxes(small[..., SMALL_FF:SMALL_FF + 2 * N_HEADS], 1, 2)
        f_t = jnp.pad(small_t[:, :N_HEADS], ((0, 0), (0, 8 - N_HEADS), (0, 0)))
        iw_t = jnp.pad(small_t[:, N_HEADS:], ((0, 0), (0, 8 - N_HEADS), (0, 0)))
        bf_col = jnp.pad(b_forget[l], (0, 8 - N_HEADS)).reshape(8, 1)
        c_parts = _forget_cumsum(f_t, bf_col)[:, :, :N_HEADS]

        pool_wbd = jnp.zeros((BRANCH_W, BRANCH_W), F32)
        for g in range(len(POOL_WINDOWS)):
            sl = slice(g * POOL_GROUP, (g + 1) * POOL_GROUP)
            pool_wbd = pool_wbd.at[sl, sl].set(pool_w[l, g])
        br_a, br_d = _local_mixers(pool_v, glu, pool_wbd.astype(BF16), pool_scale[l], conv_dw[l],
                                   conv_b[l], conv_ln_g[l], conv_ln_b[l], t["local_ts"])
        br_b = _fox_attention(fq, fk, fv, c_parts, t["fox_tq"], t["fox_ck"])
        br_c = _dsa_attention(dq, iq, iw_t, ik, dk, dv, t["dsa_tq"], t["dsa_ck"])
        br_b, br_c = jnp.swapaxes(br_b, 1, 2), jnp.swapaxes(br_c, 1, 2)

        branches = [br.reshape(n, BRANCH_W) for br in (br_a, br_b, br_c, br_d)]
        h = _merge(h, branches, w_in[l][:, GATE_OFF:].astype(BF16), b_gate[l],
                   w_branch[l].astype(BF16), w_out[l].astype(BF16), ln1_g[l], ln1_b[l], alpha,
                   t["merge_tm"])

        w_router_t = jnp.concatenate(
            [router_g[l], router_e[l], jnp.zeros((d, LANES - N_GROUPS - N_EXPERTS), F32)],
            axis=1).T.astype(BF16)
        b_router_t = jnp.concatenate(
            [router_g_b[l], router_e_b[l], jnp.zeros((LANES - N_GROUPS - N_EXPERTS,), F32)]
        ).reshape(LANES, 1)
        wg_t = jnp.swapaxes(expert_w_gate[l].astype(BF16), 1, 2)
        wu_t = jnp.swapaxes(expert_w_up[l].astype(BF16), 1, 2)
        wd_t = jnp.swapaxes(expert_w_down[l].astype(BF16), 1, 2)
        h = _moe(h, w_router_t, b_router_t, wg_t, wu_t, wd_t, ln2_g[l], ln2_b[l], alpha,
                 t["moe_tm"])
    return h.reshape(b, s, d)
```

```python
import functools

import jax
import jax.numpy as jnp
from jax import lax
from jax.experimental import pallas as pl
from jax.experimental.pallas import tpu as pltpu

F32 = jnp.float32
BF16 = jnp.bfloat16
I32 = jnp.int32

N_BRANCH = 4
HEAD_DIM = 64
N_HEADS = 4
BRANCH_W = N_HEADS * HEAD_DIM
POOL_WINDOWS = (2, 4, 8, 16)
POOL_GROUP = BRANCH_W // len(POOL_WINDOWS)
CONV_W = 31
DSA_TOPK = 256
N_GROUPS = 4
EXPERTS_PER_GROUP = 4
N_EXPERTS = N_GROUPS * EXPERTS_PER_GROUP
LN_EPS = 1e-5
NEG_INF = -1e30
ATTN_SCALE = HEAD_DIM ** -0.5

LANES = 128
HALO = 32
VMEM_LIMIT = 56 * 1024 * 1024
MOE_VMEM_LIMIT = 60 * 1024 * 1024

_OFF = {}
_o = 0
for _name, _n in (("pool", BRANCH_W), ("fq", BRANCH_W), ("fk", BRANCH_W), ("fv", BRANCH_W),
                  ("ff", N_HEADS), ("dq", BRANCH_W), ("dk", HEAD_DIM), ("dv", HEAD_DIM),
                  ("iq", BRANCH_W), ("ik", HEAD_DIM), ("iw", N_HEADS), ("conv", 2 * BRANCH_W)):
    _OFF[_name] = (_o, _o + _n)
    _o += _n
GATE_OFF = _o
SMALL_FF = HEAD_DIM
SMALL_IW = HEAD_DIM + N_HEADS
AUG = 2 * HEAD_DIM
POS_DIGIT_BITS = 6


def _cparams(*sem):
    return pltpu.CompilerParams(dimension_semantics=sem, vmem_limit_bytes=VMEM_LIMIT)


def _ln_rows(x, g, b):
    mu = jnp.mean(x, axis=-1, keepdims=True)
    xc = x - mu
    var = jnp.mean(xc * xc, axis=-1, keepdims=True)
    return xc * lax.rsqrt(var + LN_EPS) * g + b


def _sigmoid(x):
    return 1.0 / (1.0 + jnp.exp(-x))


def _dot(a, b):
    return jnp.dot(a, b, preferred_element_type=F32)


def _dot_nt(a, b):
    return lax.dot_general(a, b, (((1,), (1,)), ((), ())), preferred_element_type=F32)


def _ln_kernel(x_ref, g_ref, b_ref, o_ref):
    o_ref[...] = _ln_rows(x_ref[...], g_ref[...], b_ref[...])


def _layer_norm(x2d, g, b, tm):
    n, d = x2d.shape
    return pl.pallas_call(
        _ln_kernel,
        grid=(n // tm,),
        in_specs=[pl.BlockSpec((tm, d), lambda i: (i, 0)),
                  pl.BlockSpec((1, d), lambda i: (0, 0)),
                  pl.BlockSpec((1, d), lambda i: (0, 0))],
        out_specs=pl.BlockSpec((tm, d), lambda i: (i, 0)),
        out_shape=jax.ShapeDtypeStruct((n, d), F32),
        compiler_params=_cparams("parallel"),
        name="ln_in",
    )(x2d, g.reshape(1, d), b.reshape(1, d))


_PK = {"pool": 0, "fq": 256, "fk": 512, "fv": 768, "dq": 1024, "iq": 1280, "conv": 1536,
       "dkv": 2048, "small": 2176}
PK_COLS = 2304


def _pack_proj_weight(w):
    d = w.shape[0]
    sl = lambda k: w[:, _OFF[k][0]:_OFF[k][1]]
    small = jnp.concatenate([sl("ik"), sl("ff"), sl("iw"),
                             jnp.zeros((d, LANES - HEAD_DIM - 2 * N_HEADS), w.dtype)], axis=1)
    packed = jnp.concatenate([sl("pool"), sl("fq") * ATTN_SCALE, sl("fk"), sl("fv"),
                              sl("dq") * ATTN_SCALE, sl("iq"), sl("conv"), sl("dk"), sl("dv"),
                              small], axis=1)
    return packed.astype(BF16)


def _proj_kernel(h_ref, w_ref, pool_o, fq_o, fk_o, fv_o, dq_o, iq_o, glu_o, dk_o, dv_o, ik_o,
                 small_o):
    hb = h_ref[0].astype(BF16)
    tm = hb.shape[0]

    def piece(name, width):
        return _dot(hb, w_ref[:, _PK[name]:_PK[name] + width])

    lane = lax.broadcasted_iota(I32, (tm, HEAD_DIM), 1)
    pos = pl.program_id(1) * tm + lax.broadcasted_iota(I32, (tm, HEAD_DIM), 0)
    p_hi = (pos >> POS_DIGIT_BITS).astype(F32)
    p_lo = (pos & ((1 << POS_DIGIT_BITS) - 1)).astype(F32)
    digit = float(1 << POS_DIGIT_BITS)

    def bias_cols(c0, c1, c2, c3):
        cols = jnp.where(lane == 0, c0, jnp.where(lane == 1, c1, jnp.where(
            lane == 2, c2, jnp.where(lane == 3, c3, 0.0))))
        return cols.astype(BF16)

    def widen(r, cols):
        return jnp.concatenate([r, cols], axis=1)

    pool_o[0] = piece("pool", BRANCH_W)
    for name, out in (("fq", fq_o), ("fk", fk_o), ("fv", fv_o), ("dq", dq_o), ("iq", iq_o)):
        r = piece(name, BRANCH_W).astype(BF16)
        for h in range(N_HEADS):
            r_h = r[:, h * HEAD_DIM:(h + 1) * HEAD_DIM]
            if name == "fq":
                r_h = widen(r_h, bias_cols(-1.0, -1.0, -1.0, 0.0))
            elif name == "dq":
                slope = 2.0 ** (-8.0 * (h + 1) / N_HEADS)
                r_h = widen(r_h, bias_cols(digit * slope, slope, -digit * slope * p_hi,
                                           -slope * p_lo))
            out[0, h] = r_h
    a = piece("conv", 2 * BRANCH_W)
    glu_o[0] = a[:, :BRANCH_W] * _sigmoid(a[:, BRANCH_W:])
    kv = piece("dkv", 2 * HEAD_DIM).astype(BF16)
    dk_o[0] = widen(kv[:, :HEAD_DIM], bias_cols(p_hi, p_lo, 1.0, 1.0))
    dv_o[0] = kv[:, HEAD_DIM:]
    sm = piece("small", LANES)
    small_o[0] = sm
    ik_o[0] = sm[:, :HEAD_DIM].astype(BF16)


def _projection(h, wp, tm):
    b, s, d = h.shape
    assert s <= (1 << POS_DIGIT_BITS) ** 2, "positions are carried as two digits"
    heads = lambda w: pl.BlockSpec((1, N_HEADS, tm, w), lambda bi, i: (bi, 0, i, 0))
    rows = lambda w: pl.BlockSpec((1, tm, w), lambda bi, i: (bi, i, 0))
    head_shape = lambda w: jax.ShapeDtypeStruct((b, N_HEADS, s, w), BF16)
    hd, aug = HEAD_DIM, AUG
    return pl.pallas_call(
        _proj_kernel,
        grid=(b, s // tm),
        in_specs=[rows(d), pl.BlockSpec((d, PK_COLS), lambda bi, i: (0, 0))],
        out_specs=[rows(BRANCH_W), heads(aug), heads(hd), heads(hd), heads(aug), heads(hd),
                   rows(BRANCH_W), rows(aug), rows(hd), rows(hd), rows(LANES)],
        out_shape=[jax.ShapeDtypeStruct((b, s, BRANCH_W), F32), head_shape(aug), head_shape(hd),
                   head_shape(hd), head_shape(aug), head_shape(hd),
                   jax.ShapeDtypeStruct((b, s, BRANCH_W), F32),
                   jax.ShapeDtypeStruct((b, s, AUG), BF16),
                   jax.ShapeDtypeStruct((b, s, HEAD_DIM), BF16),
                   jax.ShapeDtypeStruct((b, s, HEAD_DIM), BF16),
                   jax.ShapeDtypeStruct((b, s, LANES), F32)],
        compiler_params=_cparams("parallel", "parallel"),
        name="proj",
    )(h, wp)


def _forget_cumsum_kernel(f_ref, b_ref, o_ref):
    s = f_ref.shape[2]
    x = f_ref[0] + b_ref[...]
    ls = jnp.minimum(x, 0.0) - jnp.log(1.0 + jnp.exp(-jnp.abs(x)))
    r = lax.broadcasted_iota(I32, (LANES, LANES), 0)
    c = lax.broadcasted_iota(I32, (LANES, LANES), 1)
    upper = jnp.where(r <= c, 1.0, 0.0).astype(F32)
    carry = jnp.zeros((x.shape[0], 1), F32)
    for j in range(s // LANES):
        blk = jnp.dot(ls[:, j * LANES:(j + 1) * LANES], upper, preferred_element_type=F32,
                      precision=lax.Precision.HIGHEST) + carry
        carry = blk[:, LANES - 1:LANES]
        rest = blk
        for piece in range(3):
            part = rest.astype(BF16)
            o_ref[0, piece, :, j * LANES:(j + 1) * LANES] = part
            rest = rest - part.astype(F32)


def _forget_cumsum(f_t, b_col):
    b, r, s = f_t.shape
    return pl.pallas_call(
        _forget_cumsum_kernel,
        grid=(b,),
        in_specs=[pl.BlockSpec((1, r, s), lambda bi: (bi, 0, 0)),
                  pl.BlockSpec((r, 1), lambda bi: (0, 0))],
        out_specs=pl.BlockSpec((1, 3, r, s), lambda bi: (bi, 0, 0, 0)),
        out_shape=jax.ShapeDtypeStruct((b, 3, r, s), BF16),
        compiler_params=_cparams("parallel"),
        name="forget_cumsum",
    )(f_t, b_col)


def _local_kernel(pv_ref, pvh_ref, gl_ref, glh_ref, pw_ref, ps_ref, dw_ref, cb_ref, g_ref, b_ref,
                  a_o, d_o, pbuf, cbuf, shifted, *, ts):
    i = pl.program_id(1)
    first = i == 0
    pbuf[0:HALO, :] = jnp.where(first, 0.0, pvh_ref[0])
    pbuf[HALO:HALO + ts, :] = pv_ref[0]
    cbuf[0:HALO, :] = jnp.where(first, 0.0, glh_ref[0])
    cbuf[HALO:HALO + ts, :] = gl_ref[0]

    def shifted_window(buf, back):
        whole, r = divmod(back, SUBLANES)
        lo = HALO - whole * SUBLANES
        return buf[lo:lo + ts, :] if r == 0 else shifted[r - 1, lo:lo + ts, :]

    def fill_shifted(buf):
        for r in range(1, SUBLANES):
            shifted[r - 1, SUBLANES:, :] = buf[SUBLANES - r:HALO + ts - r, :]

    fill_shifted(pbuf)
    lane = lax.broadcasted_iota(I32, (ts, BRANCH_W), 1)
    pos = (i * ts + 1 + lax.broadcasted_iota(I32, (ts, 1), 0)).astype(F32)
    v = pbuf[HALO:HALO + ts, :]
    run = v
    pooled = jnp.zeros((ts, BRANCH_W), F32)
    shift = 1
    for g, w in enumerate(POOL_WINDOWS):
        while shift < w:
            run = run + shifted_window(pbuf, shift)
            shift += 1
        in_group = (lane >= g * POOL_GROUP) & (lane < (g + 1) * POOL_GROUP)
        pooled = jnp.where(in_group, run / jnp.minimum(pos, float(w)), pooled)
    pooled = pooled - v
    a_o[0] = (_dot(pooled.astype(BF16), pw_ref[...]) * ps_ref[...]).astype(BF16)

    fill_shifted(cbuf)
    y = jnp.zeros((ts, BRANCH_W), F32) + cb_ref[...]
    for j in range(CONV_W):
        y = y + shifted_window(cbuf, CONV_W - 1 - j) * dw_ref[j:j + 1, :]
    z = _ln_rows(y, g_ref[...], b_ref[...])
    d_o[0] = (z * _sigmoid(z)).astype(BF16)


def _local_mixers(pool_v, glu, pool_wbd, pool_scale, conv_dw, conv_b, ln_g, ln_b, ts):
    b, s, c = pool_v.shape
    cur = pl.BlockSpec((1, ts, c), lambda bi, i: (bi, i, 0))
    halo = pl.BlockSpec((1, HALO, c), lambda bi, i: (bi, jnp.maximum(i * (ts // HALO) - 1, 0), 0))
    full2 = lambda shp: pl.BlockSpec(shp, lambda bi, i: (0, 0))
    out_shape = jax.ShapeDtypeStruct((b, s, c), BF16)
    return pl.pallas_call(
        functools.partial(_local_kernel, ts=ts),
        grid=(b, s // ts),
        in_specs=[cur, halo, cur, halo, full2((c, c)), full2((1, c)), full2((CONV_W, c)),
                  full2((1, c)), full2((1, c)), full2((1, c))],
        out_specs=[cur, cur],
        out_shape=[out_shape, out_shape],
        scratch_shapes=[pltpu.VMEM((HALO + ts, c), F32), pltpu.VMEM((HALO + ts, c), F32),
                        pltpu.VMEM((SUBLANES - 1, HALO + ts, c), F32)],
        compiler_params=_cparams("parallel", "parallel"),
        name="local_mixers",
    )(pool_v, pool_v, glu, glu, pool_wbd, pool_scale.reshape(1, c), conv_dw, conv_b.reshape(1, c),
      ln_g.reshape(1, c), ln_b.reshape(1, c))


SUBLANES = 8
FOLD_CHAINS = 8


def _fold_rows(x, op):
    slabs = [x[i:i + SUBLANES] for i in range(0, x.shape[0], SUBLANES)]
    accs = slabs[:FOLD_CHAINS]
    for i, slab in enumerate(slabs[FOLD_CHAINS:]):
        accs[i % FOLD_CHAINS] = op(accs[i % FOLD_CHAINS], slab)
    while len(accs) > 1:
        accs = [op(accs[i], accs[i + 1]) for i in range(0, len(accs) - 1, 2)] + (
            [accs[-1]] if len(accs) % 2 else [])
    return accs[0]


def _softmax_chunk(s, m, l):
    m_new = jnp.maximum(m, jnp.max(_fold_rows(s, jnp.maximum), axis=0, keepdims=True))
    alpha = jnp.exp(m - m_new)
    p = jnp.exp(s - m_new)
    l_new = alpha * l + jnp.sum(_fold_rows(p, jnp.add), axis=0, keepdims=True)
    return m_new, l_new, alpha, p.astype(BF16)


def _causal_split(qi, tq, ck):
    return (qi * tq + 1) // ck, ((qi + 1) * tq + ck - 1) // ck


CHUNK_STEPS = (4, 2, 1)


def _attention_loops(n_plain, n_chunks, tq, scores, adjust, edit_last, values):
    def consume(j, ss, carry, edit):
        stats, ps = [], []
        for h in range(N_HEADS):
            s = adjust(j, ss[h])
            if edit:
                s = edit_last(j, s)
            m_new, l_new, alpha, p = _softmax_chunk(s, carry[h][0], carry[h][1])
            stats.append((m_new, l_new, alpha))
            ps.append(p)
        out = []
        for h in range(N_HEADS):
            m_new, l_new, alpha = stats[h]
            out.append((m_new, l_new, alpha * carry[h][2] + _dot(values(j, h), ps[h])))
        return out

    def step(j0, carry, n, edit):
        ss = [scores(j0 + c) for c in range(n)]
        carry = list(carry)
        for c in range(n):
            carry = consume(j0 + c, ss[c], carry, edit)
        return tuple(carry)

    carry = tuple((jnp.full((1, tq), NEG_INF, F32), jnp.zeros((1, tq), F32),
                   jnp.zeros((HEAD_DIM, tq), F32)) for _ in range(N_HEADS))
    done = 0
    for width in CHUNK_STEPS:
        n_steps = (n_plain - done) // width
        carry = lax.fori_loop(
            0, n_steps, lambda i, c, w=width, d0=done: step(d0 + i * w, c, w, False), carry)
        done = done + n_steps * width
    if edit_last is None:
        return carry
    return lax.fori_loop(n_plain, n_chunks, lambda j, c: step(j, c, 1, True), carry)


def _fox_kernel(q_ref, k_ref, vt_ref, o_ref, *, tq, ck):
    qi = pl.program_id(1)
    n_full, n_chunks = _causal_split(qi, tq, ck)
    qpos = qi * tq + lax.broadcasted_iota(I32, (1, tq), 1)
    kiota = lax.broadcasted_iota(I32, (ck, 1), 0)
    qs = [q_ref[0, h] for h in range(N_HEADS)]

    def scores(j):
        start = pl.multiple_of(j * ck, ck)
        return [_dot_nt(k_ref[0, h, pl.ds(start, ck), :], qs[h]) for h in range(N_HEADS)]

    def causal(j, s):
        return jnp.where(j * ck + kiota <= qpos, s, NEG_INF)

    def values(j, h):
        return vt_ref[0, h, j]

    carry = _attention_loops(n_full, n_chunks, tq, scores, lambda j, s: s, causal, values)
    for h in range(N_HEADS):
        _, l, acc = carry[h]
        o_ref[0, h * HEAD_DIM:(h + 1) * HEAD_DIM, :] = (acc / l).astype(BF16)


def _fox_attention(q_aug, fk, fv, c_parts, tq, ck):
    b, nh, s, dh = fk.shape
    n_ck = s // ck
    pad = jnp.zeros((b, nh, s, AUG - dh - 3), BF16)
    k_aug = jnp.concatenate([fk, jnp.moveaxis(c_parts, 1, 3), pad], axis=-1)
    v_t = jnp.swapaxes(fv.reshape(b, nh, n_ck, ck, dh), 3, 4)
    return pl.pallas_call(
        functools.partial(_fox_kernel, tq=tq, ck=ck),
        grid=(b, s // tq),
        in_specs=[pl.BlockSpec((1, nh, tq, AUG), lambda bi, i: (bi, 0, i, 0)),
                  pl.BlockSpec((1, nh, s, AUG), lambda bi, i: (bi, 0, 0, 0)),
                  pl.BlockSpec((1, nh, n_ck, dh, ck), lambda bi, i: (bi, 0, 0, 0, 0))],
        out_specs=pl.BlockSpec((1, nh * dh, tq), lambda bi, i: (bi, 0, i)),
        out_shape=jax.ShapeDtypeStruct((b, nh * dh, s), BF16),
        compiler_params=_cparams("parallel", "arbitrary"),
        name="fox_attention",
    )(q_aug, k_aug, v_t)


def _key_to_f32(key):
    bits = key ^ (lax.shift_right_arithmetic(key, 31) & 0x7FFFFFFF)
    return lax.bitcast_convert_type(bits, F32)


def _f32_to_key(x):
    bits = lax.bitcast_convert_type(x, I32)
    return bits ^ (lax.shift_right_arithmetic(bits, 31) & 0x7FFFFFFF)


KEY_POS_INF = 0x7F800000
KEY_MIN_NORMAL = 0x00800000
KEY_NEG_INF = 0x007FFFFF - 2 ** 31
PROBE_STEP = 1 << 24
MAX_PROBES = 36
PROBES_PER_CHECK = 3
COUNT_ROWS = 64


def _dsa_kernel(dq_ref, iq_ref, iwt_ref, ik_ref, dk_ref, dvt_ref, o_ref, score_buf, bias_buf, *,
                tq, ck, topk):
    qi = pl.program_id(1)
    n_full, n_chunks = _causal_split(qi, tq, ck)
    qpos = qi * tq + lax.broadcasted_iota(I32, (1, tq), 1)
    kiota = lax.broadcasted_iota(I32, (ck, 1), 0)
    w_idx = iwt_ref[0]
    kf = float(topk)

    def count(pred):
        def cb(j, acc):
            for r0 in range(0, ck, COUNT_ROWS):
                blk = score_buf[j, r0:r0 + COUNT_ROWS, :]
                acc = acc + _fold_rows(jnp.where(pred(blk), 1.0, 0.0), jnp.add)
            return acc
        acc = lax.fori_loop(0, n_chunks, cb, jnp.zeros((SUBLANES, tq), F32))
        return jnp.sum(acc, axis=0, keepdims=True)

    def score_body(j, stats, masked):
        top, n_pos, n_nn = stats
        start = pl.multiple_of(j * ck, ck)
        ikc = ik_ref[0, pl.ds(start, ck), :]
        sc = None
        for h in range(N_HEADS):
            term = jnp.maximum(_dot_nt(ikc, iq_ref[0, h]), 0.0) * w_idx[h:h + 1, :]
            sc = term if sc is None else sc + term
        if masked:
            sc = jnp.where(start + kiota <= qpos, sc, NEG_INF)
        score_buf[j] = sc
        return (jnp.maximum(top, _fold_rows(sc, jnp.maximum)),
                n_pos + _fold_rows(jnp.where(sc > 0.0, 1.0, 0.0), jnp.add),
                n_nn + _fold_rows(jnp.where(sc >= 0.0, 1.0, 0.0), jnp.add))

    stats = (jnp.full((SUBLANES, tq), -jnp.inf, F32), jnp.zeros((SUBLANES, tq), F32),
             jnp.zeros((SUBLANES, tq), F32))
    stats = lax.fori_loop(0, n_full, functools.partial(score_body, masked=False), stats)
    stats = lax.fori_loop(n_full, n_chunks, functools.partial(score_body, masked=True), stats)

    top_key = _f32_to_key(jnp.max(stats[0], axis=0, keepdims=True))
    c_pos = jnp.sum(stats[1], axis=0, keepdims=True)
    c_nn = jnp.sum(stats[2], axis=0, keepdims=True)
    pos_side, zero_tie = c_pos >= kf, c_nn >= kf
    n_tot = (n_chunks * ck).astype(F32)
    lo = jnp.where(pos_side, KEY_MIN_NORMAL, jnp.where(zero_tie, 0, KEY_NEG_INF)).astype(I32)
    hi = jnp.where(pos_side, KEY_POS_INF, jnp.where(zero_tie, 1, 0)).astype(I32)
    c_lo = jnp.where(pos_side, c_pos, jnp.where(zero_tie, c_nn, n_tot))
    c_hi = jnp.where(pos_side, 0.0, jnp.where(zero_tie, c_pos, c_nn))

    def unsettled(lo, hi, c_lo):
        return (c_lo != kf) & (hi > lo + 1)

    def probe(state):
        it, lo, hi, c_lo, c_hi = state
        mid = (lo >> 1) + (hi >> 1) + (lo & hi & 1)
        near_top = jnp.where(hi > 0, jnp.maximum(mid, hi - PROBE_STEP), mid)
        cand = jnp.where(it == 0, top_key, jnp.where(it == 1, near_top, mid))
        cand = jnp.minimum(jnp.maximum(cand, lo + 1), hi - 1)
        cand_f = _key_to_f32(cand)
        c = count(lambda s: s >= cand_f)
        live = unsettled(lo, hi, c_lo)
        up = live & (c >= kf)
        down = live & (c < kf)
        return (it + 1, jnp.where(up, cand, lo), jnp.where(down, cand, hi),
                jnp.where(up, c, c_lo), jnp.where(down, c, c_hi))

    def searching(state):
        it, lo, hi, c_lo, _ = state
        left = jnp.max(jnp.where(unsettled(lo, hi, c_lo), 1.0, 0.0))
        return (it < MAX_PROBES) & (left > 0.0)

    def probes(state):
        for _ in range(PROBES_PER_CHECK):
            state = probe(state)
        return state

    _, lo, hi, _, c_hi = lax.while_loop(searching, probes, (jnp.int32(0), lo, hi, c_lo, c_hi))
    thr = _key_to_f32(lo)

    need = jnp.where(hi == lo + 1, kf - c_hi, float(ck) * n_tot)
    r_i = lax.broadcasted_iota(I32, (ck, ck), 0)
    c_i = lax.broadcasted_iota(I32, (ck, ck), 1)
    lower = jnp.where(c_i <= r_i, 1.0, 0.0).astype(BF16)

    def bias_body(j, seen, masked):
        start = pl.multiple_of(j * ck, ck)
        groups = range(0, ck, COUNT_ROWS)
        ties = jnp.concatenate(
            [jnp.where(score_buf[j, r0:r0 + COUNT_ROWS, :] == thr, 1.0, 0.0).astype(BF16)
             for r0 in groups], axis=0)
        rank = _dot(lower, ties) + seen
        for r0 in groups:
            rows = slice(r0, r0 + COUNT_ROWS)
            sc = score_buf[j, rows, :]
            keep = jnp.where(sc > thr, 0.0, jnp.where(
                sc == thr, jnp.where(rank[rows] <= need, 0.0, NEG_INF), NEG_INF))
            if masked:
                keep = jnp.where(start + kiota[rows] <= qpos, keep, NEG_INF)
            bias_buf[j, rows, :] = keep
        return rank[ck - 1:ck, :]

    seen = lax.fori_loop(0, n_full, functools.partial(bias_body, masked=False),
                         jnp.zeros((1, tq), F32))
    lax.fori_loop(n_full, n_chunks, functools.partial(bias_body, masked=True), seen)

    qs = [dq_ref[0, h] for h in range(N_HEADS)]

    def scores(j):
        start = pl.multiple_of(j * ck, ck)
        kc = dk_ref[0, pl.ds(start, ck), :]
        bias = bias_buf[j]
        return [_dot_nt(kc, qs[h]) + bias for h in range(N_HEADS)]

    carry = _attention_loops(n_chunks, n_chunks, tq, scores, lambda j, s: s, None,
                             lambda j, h: dvt_ref[0, j])
    for h in range(N_HEADS):
        _, l, acc = carry[h]
        o_ref[0, h * HEAD_DIM:(h + 1) * HEAD_DIM, :] = (acc / l).astype(BF16)


def _dsa_attention(q_aug, iq, iw_t, ik, k_aug, dv, tq, ck):
    b, nh, s, dh = iq.shape
    topk = min(DSA_TOPK, s // 4)
    assert ck >= topk, "every query must see at least topk (possibly masked) candidates"
    n_ck = s // ck
    v_t = jnp.swapaxes(dv.reshape(b, n_ck, ck, dh), 2, 3)
    return pl.pallas_call(
        functools.partial(_dsa_kernel, tq=tq, ck=ck, topk=topk),
        grid=(b, s // tq),
        in_specs=[pl.BlockSpec((1, nh, tq, AUG), lambda bi, i: (bi, 0, i, 0)),
                  pl.BlockSpec((1, nh, tq, dh), lambda bi, i: (bi, 0, i, 0)),
                  pl.BlockSpec((1, 8, tq), lambda bi, i: (bi, 0, i)),
                  pl.BlockSpec((1, s, dh), lambda bi, i: (bi, 0, 0)),
                  pl.BlockSpec((1, s, AUG), lambda bi, i: (bi, 0, 0)),
                  pl.BlockSpec((1, n_ck, dh, ck), lambda bi, i: (bi, 0, 0, 0))],
        out_specs=pl.BlockSpec((1, nh * dh, tq), lambda bi, i: (bi, 0, i)),
        out_shape=jax.ShapeDtypeStruct((b, nh * dh, s), BF16),
        scratch_shapes=[pltpu.VMEM((n_ck, ck, tq), F32), pltpu.VMEM((n_ck, ck, tq), F32)],
        compiler_params=_cparams("parallel", "arbitrary"),
        name="dsa_attention",
    )(q_aug, iq, iw_t, ik, k_aug, v_t)


def _merge_kernel(h_ref, a_ref, b_ref, c_ref, d_ref, wg_ref, bg_ref, wb_ref, wo_ref, g_ref,
                  beta_ref, o_ref, *, alpha):
    h = h_ref[...]
    hb = h.astype(BF16)
    d = h.shape[1]
    acc = jnp.zeros(h.shape, F32)
    for n, br in enumerate((a_ref, b_ref, c_ref, d_ref)):
        gate = _sigmoid(_dot(hb, wg_ref[:, n * d:(n + 1) * d]) + bg_ref[:, n * d:(n + 1) * d])
        acc = acc + gate * _dot(br[...], wb_ref[n])
    mixed = _dot(acc.astype(BF16), wo_ref[...])
    o_ref[...] = _ln_rows(alpha * h + mixed, g_ref[...], beta_ref[...])


def _merge(h2d, branches, w_gate, b_gate, w_branch, w_out, ln_g, ln_b, alpha, tm):
    n, d = h2d.shape
    c = branches[0].shape[1]
    rows = lambda w: pl.BlockSpec((tm, w), lambda i: (i, 0))
    const = lambda shp: pl.BlockSpec(shp, lambda i: (0,) * len(shp))
    return pl.pallas_call(
        functools.partial(_merge_kernel, alpha=alpha),
        grid=(n // tm,),
        in_specs=[rows(d), rows(c), rows(c), rows(c), rows(c), const((d, N_BRANCH * d)),
                  const((1, N_BRANCH * d)), const((N_BRANCH, c, d)), const((d, d)),
                  const((1, d)), const((1, d))],
        out_specs=rows(d),
        out_shape=jax.ShapeDtypeStruct((n, d), F32),
        compiler_params=_cparams("parallel"),
        name="merge",
    )(h2d, *branches, w_gate, b_gate.reshape(1, -1), w_branch, w_out, ln_g.reshape(1, d),
      ln_b.reshape(1, d))


ROUTER_E0 = N_GROUPS
MOE_BLK = 256


def _routing_t(logits):
    row = lax.broadcasted_iota(I32, logits.shape, 0)
    far = jnp.int32(4 * LANES)
    neg = -jnp.inf

    def first_max(x):
        mx = jnp.max(x, axis=0, keepdims=True)
        idx = jnp.min(jnp.where(x == mx, row, far), axis=0, keepdims=True)
        return mx, idx

    lg = jnp.where(row < N_GROUPS, logits, neg)
    g_max, g_idx = first_max(lg)
    p_group = 1.0 / jnp.sum(jnp.exp(lg - g_max), axis=0, keepdims=True)
    lo = ROUTER_E0 + EXPERTS_PER_GROUP * g_idx
    le = jnp.where((row >= lo) & (row < lo + EXPERTS_PER_GROUP), logits, neg)
    m1, e1 = first_max(le)
    m2, e2 = first_max(jnp.where(row == e1, neg, le))
    r = jnp.exp(m2 - m1)
    gate1 = p_group * (1.0 / (1.0 + r))
    gate2 = p_group * (r / (1.0 + r))
    return jnp.where(row == e1, gate1, jnp.where(row == e2, gate2, 0.0)), g_idx


def _bf16_pieces(x, n):
    out, rest = [], x
    for _ in range(n):
        part = rest.astype(BF16)
        out.append(part)
        rest = rest - part.astype(F32)
    return out


def _moe_kernel(h_ref, wr_ref, br_ref, wg_ref, wu_ref, wd_ref, g_ref, beta_ref, o_ref, blk_ref, p_ref,
                xs_ref, gs_ref, acc_ref, *, alpha):
    e = pl.program_id(1)
    t, d = h_ref.shape
    f = wd_ref.shape[2]
    n_slots = p_ref.shape[0] // MOE_BLK

    @pl.when(e == 0)
    def _():
        h = h_ref[...]
        hb = h.astype(BF16)
        gates, g_idx = _routing_t(_dot_nt(wr_ref[...], hb) + br_ref[...])
        grp = lax.broadcasted_iota(I32, (SUBLANES, t), 0)
        onehot = jnp.where(grp == g_idx, 1.0, 0.0)
        r_i = lax.broadcasted_iota(I32, (t, t), 0)
        c_i = lax.broadcasted_iota(I32, (t, t), 1)
        before = jnp.where(r_i < c_i, 1.0, 0.0).astype(BF16)
        rank = _dot(onehot.astype(BF16), before)
        total = jnp.sum(onehot, axis=1, keepdims=True)
        start = jnp.int32(0)
        starts = jnp.zeros((SUBLANES, 1), F32)
        grp_col = lax.broadcasted_iota(I32, (SUBLANES, 1), 0)
        for g in range(N_GROUPS):
            n_g = (total[g, 0].astype(I32) + (MOE_BLK - 1)) // MOE_BLK
            blk_ref[2 * g] = start
            blk_ref[2 * g + 1] = n_g
            starts = jnp.where(grp_col == g, (start * MOE_BLK).astype(F32), starts)
            start = start + n_g
        blk_ref[2 * N_GROUPS] = start
        pos = jnp.sum(onehot * (starts + rank), axis=0, keepdims=True).astype(I32)
        slot = lax.broadcasted_iota(I32, (p_ref.shape[0], t), 0)
        p_ref[...] = jnp.where(slot == pos, 1.0, 0.0).astype(BF16)
        perm = p_ref[...]
        xs = _dot_nt(h.T.astype(BF16), perm).astype(BF16)
        gs = sum(_dot_nt(piece, perm) for piece in _bf16_pieces(gates, 3))
        for b in range(n_slots):
            cols = slice(b * MOE_BLK, (b + 1) * MOE_BLK)
            xs_ref[b] = xs[:, cols]
            acc_ref[b] = jnp.zeros((d, MOE_BLK), F32)
            for k in range(N_EXPERTS):
                gs_ref[b, k] = gs[ROUTER_E0 + k:ROUTER_E0 + k + 1, cols]

    grp_e = e // EXPERTS_PER_GROUP
    blk0 = blk_ref[2 * grp_e]

    def expert_block(i, _):
        b = blk0 + i
        xs = xs_ref[b]
        hg = _dot(wg_ref[0], xs)
        hu = _dot(wu_ref[0], xs)
        hid = (hg * _sigmoid(hg) * hu).astype(BF16)
        y = jnp.concatenate([_dot(wd_ref[0, :d // 2], hid), _dot(wd_ref[0, d // 2:], hid)], axis=0)
        acc_ref[b] += gs_ref[b, e] * y
        return 0

    lax.fori_loop(0, blk_ref[2 * grp_e + 1], expert_block, 0)

    @pl.when(e == pl.num_programs(1) - 1)
    def _():
        pieces = [jnp.concatenate(col, axis=1) for col in
                  zip(*[_bf16_pieces(acc_ref[b], 2) for b in range(n_slots)])]
        half = t // 2
        for c in range(2):
            cols = slice(c * half, (c + 1) * half)
            out_t = sum(_dot(piece, p_ref[:, cols]) for piece in pieces)
            o_ref[cols, :] = _ln_rows(alpha * h_ref[cols, :] + out_t.T, g_ref[...], beta_ref[...])


def _moe(h2d, w_router_t, b_router_t, wg_t, wu_t, wd_t, ln_g, ln_b, alpha, tm):
    n, d = h2d.shape
    ne, _, f = wd_t.shape
    assert tm % MOE_BLK == 0
    n_slots = tm // MOE_BLK + N_GROUPS
    rows = pl.BlockSpec((tm, d), lambda i, e: (i, 0))
    const = lambda shp: pl.BlockSpec(shp, lambda i, e: (0,) * len(shp))
    return pl.pallas_call(
        functools.partial(_moe_kernel, alpha=alpha),
        grid=(n // tm, ne),
        in_specs=[rows, const((LANES, d)), const((LANES, 1)),
                  pl.BlockSpec((1, f, d), lambda i, e: (e, 0, 0)),
                  pl.BlockSpec((1, f, d), lambda i, e: (e, 0, 0)),
                  pl.BlockSpec((1, d, f), lambda i, e: (e, 0, 0)),
                  const((1, d)), const((1, d))],
        out_specs=rows,
        out_shape=jax.ShapeDtypeStruct((n, d), F32),
        scratch_shapes=[pltpu.SMEM((2 * N_GROUPS + 1,), I32),
                        pltpu.VMEM((n_slots * MOE_BLK, tm), BF16),
                        pltpu.VMEM((n_slots, d, MOE_BLK), BF16),
                        pltpu.VMEM((n_slots, ne, 1, MOE_BLK), F32),
                        pltpu.VMEM((n_slots, d, MOE_BLK), F32)],
        compiler_params=pltpu.CompilerParams(dimension_semantics=("parallel", "arbitrary"),
                                             vmem_limit_bytes=MOE_VMEM_LIMIT),
        name="moe",
    )(h2d, w_router_t, b_router_t, wg_t, wu_t, wd_t, ln_g.reshape(1, d), ln_b.reshape(1, d))


def _tiles(b, s):
    n = b * s
    pick = lambda total, want: want if total % want == 0 else total
    return dict(
        ln_tm=pick(n, 512), proj_tm=pick(s, 512), local_ts=pick(s, 512),
        fox_tq=pick(s, 256), fox_ck=pick(s, 512), dsa_tq=pick(s, 256), dsa_ck=pick(s, 512),
        merge_tm=pick(n, 512), moe_tm=pick(n, 1024))


def kernel(x, ln_in_g, ln_in_b, w_in, b_forget, b_gate, pool_w, pool_scale, conv_dw, conv_b,
           conv_ln_g, conv_ln_b, w_branch, w_out, ln1_g, ln1_b, router_g, router_g_b, router_e,
           router_e_b, expert_w_gate, expert_w_up, expert_w_down, ln2_g, ln2_b):
    b, s, d = x.shape
    depth = w_in.shape[0]
    alpha = (2.0 * depth) ** 0.25
    t = _tiles(b, s)
    n = b * s

    h = _layer_norm(x.reshape(n, d), ln_in_g, ln_in_b, t["ln_tm"])
    for l in range(depth):
        wp = _pack_proj_weight(w_in[l])
        (pool_v, fq, fk, fv, dq, iq, glu, dk, dv, ik, small) = _projection(
            h.reshape(b, s, d), wp, t["proj_tm"])

        small_t = jnp.swapaxes(small[..., SMALL_FF:SMALL_FF + 2 * N_HEADS], 1, 2)
        f_t = jnp.pad(small_t[:, :N_HEADS], ((0, 0), (0, 8 - N_HEADS), (0, 0)))
        iw_t = jnp.pad(small_t[:, N_HEADS:], ((0, 0), (0, 8 - N_HEADS), (0, 0)))
        bf_col = jnp.pad(b_forget[l], (0, 8 - N_HEADS)).reshape(8, 1)
        c_parts = _forget_cumsum(f_t, bf_col)[:, :, :N_HEADS]

        pool_wbd = jnp.zeros((BRANCH_W, BRANCH_W), F32)
        for g in range(len(POOL_WINDOWS)):
            sl = slice(g * POOL_GROUP, (g + 1) * POOL_GROUP)
            pool_wbd = pool_wbd.at[sl, sl].set(pool_w[l, g])
        br_a, br_d = _local_mixers(pool_v, glu, pool_wbd.astype(BF16), pool_scale[l], conv_dw[l],
                                   conv_b[l], conv_ln_g[l], conv_ln_b[l], t["local_ts"])
        br_b = _fox_attention(fq, fk, fv, c_parts, t["fox_tq"], t["fox_ck"])
        br_c = _dsa_attention(dq, iq, iw_t, ik, dk, dv, t["dsa_tq"], t["dsa_ck"])
        br_b, br_c = jnp.swapaxes(br_b, 1, 2), jnp.swapaxes(br_c, 1, 2)

        branches = [br.reshape(n, BRANCH_W) for br in (br_a, br_b, br_c, br_d)]
        h = _merge(h, branches, w_in[l][:, GATE_OFF:].astype(BF16), b_gate[l],
                   w_branch[l].astype(BF16), w_out[l].astype(BF16), ln1_g[l], ln1_b[l], alpha,
                   t["merge_tm"])

        w_router_t = jnp.concatenate(
            [router_g[l], router_e[l], jnp.zeros((d, LANES - N_GROUPS - N_EXPERTS), F32)],
            axis=1).T.astype(BF16)
        b_router_t = jnp.concatenate(
            [router_g_b[l], router_e_b[l], jnp.zeros((LANES - N_GROUPS - N_EXPERTS,), F32)]
        ).reshape(LANES, 1)
        wg_t = jnp.swapaxes(expert_w_gate[l].astype(BF16), 1, 2)
        wu_t = jnp.swapaxes(expert_w_up[l].astype(BF16), 1, 2)
        wd_t = jnp.swapaxes(expert_w_down[l].astype(BF16), 1, 2)
        h = _moe(h, w_router_t, b_router_t, wg_t, wu_t, wd_t, ln2_g[l], ln2_b[l], alpha,
                 t["moe_tm"])
    return h.reshape(b, s, d)
```

```python
import functools

import jax
import jax.numpy as jnp
from jax import lax
from jax.experimental import pallas as pl
from jax.experimental.pallas import tpu as pltpu

F32 = jnp.float32
BF16 = jnp.bfloat16
I32 = jnp.int32

N_BRANCH = 4
HEAD_DIM = 64
N_HEADS = 4
BRANCH_W = N_HEADS * HEAD_DIM
POOL_WINDOWS = (2, 4, 8, 16)
POOL_GROUP = BRANCH_W // len(POOL_WINDOWS)
CONV_W = 31
DSA_TOPK = 256
N_GROUPS = 4
EXPERTS_PER_GROUP = 4
N_EXPERTS = N_GROUPS * EXPERTS_PER_GROUP
LN_EPS = 1e-5
NEG_INF = -1e30
ATTN_SCALE = HEAD_DIM ** -0.5

LANES = 128
HALO = 32
VMEM_LIMIT = 56 * 1024 * 1024
MOE_VMEM_LIMIT = 60 * 1024 * 1024

_OFF = {}
_o = 0
for _name, _n in (("pool", BRANCH_W), ("fq", BRANCH_W), ("fk", BRANCH_W), ("fv", BRANCH_W),
                  ("ff", N_HEADS), ("dq", BRANCH_W), ("dk", HEAD_DIM), ("dv", HEAD_DIM),
                  ("iq", BRANCH_W), ("ik", HEAD_DIM), ("iw", N_HEADS), ("conv", 2 * BRANCH_W)):
    _OFF[_name] = (_o, _o + _n)
    _o += _n
GATE_OFF = _o
SMALL_FF = HEAD_DIM
SMALL_IW = HEAD_DIM + N_HEADS
AUG = 2 * HEAD_DIM
POS_DIGIT_BITS = 6


def _cparams(*sem):
    return pltpu.CompilerParams(dimension_semantics=sem, vmem_limit_bytes=VMEM_LIMIT)


def _ln_rows(x, g, b):
    mu = jnp.mean(x, axis=-1, keepdims=True)
    xc = x - mu
    var = jnp.mean(xc * xc, axis=-1, keepdims=True)
    return xc * lax.rsqrt(var + LN_EPS) * g + b


def _sigmoid(x):
    return 1.0 / (1.0 + jnp.exp(-x))


def _dot(a, b):
    return jnp.dot(a, b, preferred_element_type=F32)


def _dot_nt(a, b):
    return lax.dot_general(a, b, (((1,), (1,)), ((), ())), preferred_element_type=F32)


def _ln_kernel(x_ref, g_ref, b_ref, o_ref):
    o_ref[...] = _ln_rows(x_ref[...], g_ref[...], b_ref[...])


def _layer_norm(x2d, g, b, tm):
    n, d = x2d.shape
    return pl.pallas_call(
        _ln_kernel,
        grid=(n // tm,),
        in_specs=[pl.BlockSpec((tm, d), lambda i: (i, 0)),
                  pl.BlockSpec((1, d), lambda i: (0, 0)),
                  pl.BlockSpec((1, d), lambda i: (0, 0))],
        out_specs=pl.BlockSpec((tm, d), lambda i: (i, 0)),
        out_shape=jax.ShapeDtypeStruct((n, d), F32),
        compiler_params=_cparams("parallel"),
        name="ln_in",
    )(x2d, g.reshape(1, d), b.reshape(1, d))


_PK = {"pool": 0, "fq": 256, "fk": 512, "fv": 768, "dq": 1024, "iq": 1280, "conv": 1536,
       "dkv": 2048, "small": 2176}
PK_COLS = 2304


def _pack_proj_weight(w):
    d = w.shape[0]
    sl = lambda k: w[:, _OFF[k][0]:_OFF[k][1]]
    small = jnp.concatenate([sl("ik"), sl("ff"), sl("iw"),
                             jnp.zeros((d, LANES - HEAD_DIM - 2 * N_HEADS), w.dtype)], axis=1)
    packed = jnp.concatenate([sl("pool"), sl("fq") * ATTN_SCALE, sl("fk"), sl("fv"),
                              sl("dq") * ATTN_SCALE, sl("iq"), sl("conv"), sl("dk"), sl("dv"),
                              small], axis=1)
    return packed.astype(BF16)


def _proj_kernel(h_ref, w_ref, pool_o, fq_o, fk_o, fv_o, dq_o, iq_o, glu_o, dk_o, dv_o, ik_o,
                 small_o):
    hb = h_ref[0].astype(BF16)
    tm = hb.shape[0]

    def piece(name, width):
        return _dot(hb, w_ref[:, _PK[name]:_PK[name] + width])

    lane = lax.broadcasted_iota(I32, (tm, HEAD_DIM), 1)
    pos = pl.program_id(1) * tm + lax.broadcasted_iota(I32, (tm, HEAD_DIM), 0)
    p_hi = (pos >> POS_DIGIT_BITS).astype(F32)
    p_lo = (pos & ((1 << POS_DIGIT_BITS) - 1)).astype(F32)
    digit = float(1 << POS_DIGIT_BITS)

    def bias_cols(c0, c1, c2, c3):
        cols = jnp.where(lane == 0, c0, jnp.where(lane == 1, c1, jnp.where(
            lane == 2, c2, jnp.where(lane == 3, c3, 0.0))))
        return cols.astype(BF16)

    def widen(r, cols):
        return jnp.concatenate([r, cols], axis=1)

    pool_o[0] = piece("pool", BRANCH_W)
    for name, out in (("fq", fq_o), ("fk", fk_o), ("fv", fv_o), ("dq", dq_o), ("iq", iq_o)):
        r = piece(name, BRANCH_W).astype(BF16)
        for h in range(N_HEADS):
            r_h = r[:, h * HEAD_DIM:(h + 1) * HEAD_DIM]
            if name == "fq":
                r_h = widen(r_h, bias_cols(-1.0, -1.0, -1.0, 0.0))
            elif name == "dq":
                slope = 2.0 ** (-8.0 * (h + 1) / N_HEADS)
                r_h = widen(r_h, bias_cols(digit * slope, slope, -digit * slope * p_hi,
                                           -slope * p_lo))
            out[0, h] = r_h
    a = piece("conv", 2 * BRANCH_W)
    glu_o[0] = a[:, :BRANCH_W] * _sigmoid(a[:, BRANCH_W:])
    kv = piece("dkv", 2 * HEAD_DIM).astype(BF16)
    dk_o[0] = widen(kv[:, :HEAD_DIM], bias_cols(p_hi, p_lo, 1.0, 1.0))
    dv_o[0] = kv[:, HEAD_DIM:]
    sm = piece("small", LANES)
    small_o[0] = sm
    ik_o[0] = sm[:, :HEAD_DIM].astype(BF16)


def _projection(h, wp, tm):
    b, s, d = h.shape
    assert s <= (1 << POS_DIGIT_BITS) ** 2, "positions are carried as two digits"
    heads = lambda w: pl.BlockSpec((1, N_HEADS, tm, w), lambda bi, i: (bi, 0, i, 0))
    rows = lambda w: pl.BlockSpec((1, tm, w), lambda bi, i: (bi, i, 0))
    head_shape = lambda w: jax.ShapeDtypeStruct((b, N_HEADS, s, w), BF16)
    hd, aug = HEAD_DIM, AUG
    return pl.pallas_call(
        _proj_kernel,
        grid=(b, s // tm),
        in_specs=[rows(d), pl.BlockSpec((d, PK_COLS), lambda bi, i: (0, 0))],
        out_specs=[rows(BRANCH_W), heads(aug), heads(hd), heads(hd), heads(aug), heads(hd),
                   rows(BRANCH_W), rows(aug), rows(hd), rows(hd), rows(LANES)],
        out_shape=[jax.ShapeDtypeStruct((b, s, BRANCH_W), F32), head_shape(aug), head_shape(hd),
                   head_shape(hd), head_shape(aug), head_shape(hd),
                   jax.ShapeDtypeStruct((b, s, BRANCH_W), F32),
                   jax.ShapeDtypeStruct((b, s, AUG), BF16),
                   jax.ShapeDtypeStruct((b, s, HEAD_DIM), BF16),
                   jax.ShapeDtypeStruct((b, s, HEAD_DIM), BF16),
                   jax.ShapeDtypeStruct((b, s, LANES), F32)],
        compiler_params=_cparams("parallel", "parallel"),
        name="proj",
    )(h, wp)


def _forget_cumsum_kernel(f_ref, b_ref, o_ref):
    s = f_ref.shape[2]
    x = f_ref[0] + b_ref[...]
    ls = jnp.minimum(x, 0.0) - jnp.log(1.0 + jnp.exp(-jnp.abs(x)))
    r = lax.broadcasted_iota(I32, (LANES, LANES), 0)
    c = lax.broadcasted_iota(I32, (LANES, LANES), 1)
    upper = jnp.where(r <= c, 1.0, 0.0).astype(F32)
    carry = jnp.zeros((x.shape[0], 1), F32)
    for j in range(s // LANES):
        blk = jnp.dot(ls[:, j * LANES:(j + 1) * LANES], upper, preferred_element_type=F32,
                      precision=lax.Precision.HIGHEST) + carry
        carry = blk[:, LANES - 1:LANES]
        rest = blk
        for piece in range(3):
            part = rest.astype(BF16)
            o_ref[0, piece, :, j * LANES:(j + 1) * LANES] = part
            rest = rest - part.astype(F32)


def _forget_cumsum(f_t, b_col):
    b, r, s = f_t.shape
    return pl.pallas_call(
        _forget_cumsum_kernel,
        grid=(b,),
        in_specs=[pl.BlockSpec((1, r, s), lambda bi: (bi, 0, 0)),
                  pl.BlockSpec((r, 1), lambda bi: (0, 0))],
        out_specs=pl.BlockSpec((1, 3, r, s), lambda bi: (bi, 0, 0, 0)),
        out_shape=jax.ShapeDtypeStruct((b, 3, r, s), BF16),
        compiler_params=_cparams("parallel"),
        name="forget_cumsum",
    )(f_t, b_col)


def _local_kernel(pv_ref, pvh_ref, gl_ref, glh_ref, pw_ref, ps_ref, dw_ref, cb_ref, g_ref, b_ref,
                  a_o, d_o, pbuf, cbuf, shifted, *, ts):
    i = pl.program_id(1)
    first = i == 0
    pbuf[0:HALO, :] = jnp.where(first, 0.0, pvh_ref[0])
    pbuf[HALO:HALO + ts, :] = pv_ref[0]
    cbuf[0:HALO, :] = jnp.where(first, 0.0, glh_ref[0])
    cbuf[HALO:HALO + ts, :] = gl_ref[0]

    def shifted_window(buf, back):
        whole, r = divmod(back, SUBLANES)
        lo = HALO - whole * SUBLANES
        return buf[lo:lo + ts, :] if r == 0 else shifted[r - 1, lo:lo + ts, :]

    def fill_shifted(buf):
        for r in range(1, SUBLANES):
            shifted[r - 1, SUBLANES:, :] = buf[SUBLANES - r:HALO + ts - r, :]

    fill_shifted(pbuf)
    lane = lax.broadcasted_iota(I32, (ts, BRANCH_W), 1)
    pos = (i * ts + 1 + lax.broadcasted_iota(I32, (ts, 1), 0)).astype(F32)
    v = pbuf[HALO:HALO + ts, :]
    run = v
    pooled = jnp.zeros((ts, BRANCH_W), F32)
    shift = 1
    for g, w in enumerate(POOL_WINDOWS):
        while shift < w:
            run = run + shifted_window(pbuf, shift)
            shift += 1
        in_group = (lane >= g * POOL_GROUP) & (lane < (g + 1) * POOL_GROUP)
        pooled = jnp.where(in_group, run / jnp.minimum(pos, float(w)), pooled)
    pooled = pooled - v
    a_o[0] = (_dot(pooled.astype(BF16), pw_ref[...]) * ps_ref[...]).astype(BF16)

    fill_shifted(cbuf)
    y = jnp.zeros((ts, BRANCH_W), F32) + cb_ref[...]
    for j in range(CONV_W):
        y = y + shifted_window(cbuf, CONV_W - 1 - j) * dw_ref[j:j + 1, :]
    z = _ln_rows(y, g_ref[...], b_ref[...])
    d_o[0] = (z * _sigmoid(z)).astype(BF16)


def _local_mixers(pool_v, glu, pool_wbd, pool_scale, conv_dw, conv_b, ln_g, ln_b, ts):
    b, s, c = pool_v.shape
    cur = pl.BlockSpec((1, ts, c), lambda bi, i: (bi, i, 0))
    halo = pl.BlockSpec((1, HALO, c), lambda bi, i: (bi, jnp.maximum(i * (ts // HALO) - 1, 0), 0))
    full2 = lambda shp: pl.BlockSpec(shp, lambda bi, i: (0, 0))
    out_shape = jax.ShapeDtypeStruct((b, s, c), BF16)
    return pl.pallas_call(
        functools.partial(_local_kernel, ts=ts),
        grid=(b, s // ts),
        in_specs=[cur, halo, cur, halo, full2((c, c)), full2((1, c)), full2((CONV_W, c)),
                  full2((1, c)), full2((1, c)), full2((1, c))],
        out_specs=[cur, cur],
        out_shape=[out_shape, out_shape],
        scratch_shapes=[pltpu.VMEM((HALO + ts, c), F32), pltpu.VMEM((HALO + ts, c), F32),
                        pltpu.VMEM((SUBLANES - 1, HALO + ts, c), F32)],
        compiler_params=_cparams("parallel", "parallel"),
        name="local_mixers",
    )(pool_v, pool_v, glu, glu, pool_wbd, pool_scale.reshape(1, c), conv_dw, conv_b.reshape(1, c),
      ln_g.reshape(1, c), ln_b.reshape(1, c))


SUBLANES = 8
FOLD_CHAINS = 8


def _fold_rows(x, op):
    slabs = [x[i:i + SUBLANES] for i in range(0, x.shape[0], SUBLANES)]
    accs = slabs[:FOLD_CHAINS]
    for i, slab in enumerate(slabs[FOLD_CHAINS:]):
        accs[i % FOLD_CHAINS] = op(accs[i % FOLD_CHAINS], slab)
    while len(accs) > 1:
        accs = [op(accs[i], accs[i + 1]) for i in range(0, len(accs) - 1, 2)] + (
            [accs[-1]] if len(accs) % 2 else [])
    return accs[0]


def _softmax_chunk(s, m, l):
    m_new = jnp.maximum(m, jnp.max(_fold_rows(s, jnp.maximum), axis=0, keepdims=True))
    alpha = jnp.exp(m - m_new)
    p = jnp.exp(s - m_new)
    l_new = alpha * l + jnp.sum(_fold_rows(p, jnp.add), axis=0, keepdims=True)
    return m_new, l_new, alpha, p.astype(BF16)


def _causal_split(qi, tq, ck):
    return (qi * tq + 1) // ck, ((qi + 1) * tq + ck - 1) // ck


CHUNK_STEPS = (4, 2, 1)


def _attention_loops(n_plain, n_chunks, tq, scores, adjust, edit_last, values):
    def consume(j, ss, carry, edit):
        stats, ps = [], []
        for h in range(N_HEADS):
            s = adjust(j, ss[h])
            if edit:
                s = edit_last(j, s)
            m_new, l_new, alpha, p = _softmax_chunk(s, carry[h][0], carry[h][1])
            stats.append((m_new, l_new, alpha))
            ps.append(p)
        out = []
        for h in range(N_HEADS):
            m_new, l_new, alpha = stats[h]
            out.append((m_new, l_new, alpha * carry[h][2] + _dot(values(j, h), ps[h])))
        return out

    def step(j0, carry, n, edit):
        ss = [scores(j0 + c) for c in range(n)]
        carry = list(carry)
        for c in range(n):
            carry = consume(j0 + c, ss[c], carry, edit)
        return tuple(carry)

    carry = tuple((jnp.full((1, tq), NEG_INF, F32), jnp.zeros((1, tq), F32),
                   jnp.zeros((HEAD_DIM, tq), F32)) for _ in range(N_HEADS))
    done = 0
    for width in CHUNK_STEPS:
        n_steps = (n_plain - done) // width
        carry = lax.fori_loop(
            0, n_steps, lambda i, c, w=width, d0=done: step(d0 + i * w, c, w, False), carry)
        done = done + n_steps * width
    if edit_last is None:
        return carry
    return lax.fori_loop(n_plain, n_chunks, lambda j, c: step(j, c, 1, True), carry)


def _fox_kernel(q_ref, k_ref, vt_ref, o_ref, *, tq, ck):
    qi = pl.program_id(1)
    n_full, n_chunks = _causal_split(qi, tq, ck)
    qpos = qi * tq + lax.broadcasted_iota(I32, (1, tq), 1)
    kiota = lax.broadcasted_iota(I32, (ck, 1), 0)
    qs = [q_ref[0, h] for h in range(N_HEADS)]

    def scores(j):
        start = pl.multiple_of(j * ck, ck)
        return [_dot_nt(k_ref[0, h, pl.ds(start, ck), :], qs[h]) for h in range(N_HEADS)]

    def causal(j, s):
        return jnp.where(j * ck + kiota <= qpos, s, NEG_INF)

    def values(j, h):
        return vt_ref[0, h, j]

    carry = _attention_loops(n_full, n_chunks, tq, scores, lambda j, s: s, causal, values)
    for h in range(N_HEADS):
        _, l, acc = carry[h]
        o_ref[0, h * HEAD_DIM:(h + 1) * HEAD_DIM, :] = (acc / l).astype(BF16)


def _fox_attention(q_aug, fk, fv, c_parts, tq, ck):
    b, nh, s, dh = fk.shape
    n_ck = s // ck
    pad = jnp.zeros((b, nh, s, AUG - dh - 3), BF16)
    k_aug = jnp.concatenate([fk, jnp.moveaxis(c_parts, 1, 3), pad], axis=-1)
    v_t = jnp.swapaxes(fv.reshape(b, nh, n_ck, ck, dh), 3, 4)
    return pl.pallas_call(
        functools.partial(_fox_kernel, tq=tq, ck=ck),
        grid=(b, s // tq),
        in_specs=[pl.BlockSpec((1, nh, tq, AUG), lambda bi, i: (bi, 0, i, 0)),
                  pl.BlockSpec((1, nh, s, AUG), lambda bi, i: (bi, 0, 0, 0)),
                  pl.BlockSpec((1, nh, n_ck, dh, ck), lambda bi, i: (bi, 0, 0, 0, 0))],
        out_specs=pl.BlockSpec((1, nh * dh, tq), lambda bi, i: (bi, 0, i)),
        out_shape=jax.ShapeDtypeStruct((b, nh * dh, s), BF16),
        compiler_params=_cparams("parallel", "arbitrary"),
        name="fox_attention",
    )(q_aug, k_aug, v_t)


def _key_to_f32(key):
    bits = key ^ (lax.shift_right_arithmetic(key, 31) & 0x7FFFFFFF)
    return lax.bitcast_convert_type(bits, F32)


def _f32_to_key(x):
    bits = lax.bitcast_convert_type(x, I32)
    return bits ^ (lax.shift_right_arithmetic(bits, 31) & 0x7FFFFFFF)


KEY_POS_INF = 0x7F800000
KEY_MIN_NORMAL = 0x00800000
KEY_NEG_INF = 0x007FFFFF - 2 ** 31
PROBE_STEP = 1 << 24
MAX_PROBES = 36
PROBES_PER_CHECK = 3
COUNT_ROWS = 64


def _dsa_kernel(dq_ref, iq_ref, iwt_ref, ik_ref, dk_ref, dvt_ref, o_ref, score_buf, bias_buf, *,
                tq, ck, topk):
    qi = pl.program_id(1)
    n_full, n_chunks = _causal_split(qi, tq, ck)
    qpos = qi * tq + lax.broadcasted_iota(I32, (1, tq), 1)
    kiota = lax.broadcasted_iota(I32, (ck, 1), 0)
    w_idx = iwt_ref[0]
    kf = float(topk)

    def count(pred):
        def cb(j, acc):
            for r0 in range(0, ck, COUNT_ROWS):
                blk = score_buf[j, r0:r0 + COUNT_ROWS, :]
                acc = acc + _fold_rows(jnp.where(pred(blk), 1.0, 0.0), jnp.add)
            return acc
        acc = lax.fori_loop(0, n_chunks, cb, jnp.zeros((SUBLANES, tq), F32))
        return jnp.sum(acc, axis=0, keepdims=True)

    def score_body(j, stats, masked):
        top, n_pos, n_nn = stats
        start = pl.multiple_of(j * ck, ck)
        ikc = ik_ref[0, pl.ds(start, ck), :]
        sc = None
        for h in range(N_HEADS):
            term = jnp.maximum(_dot_nt(ikc, iq_ref[0, h]), 0.0) * w_idx[h:h + 1, :]
            sc = term if sc is None else sc + term
        if masked:
            sc = jnp.where(start + kiota <= qpos, sc, NEG_INF)
        score_buf[j] = sc
        return (jnp.maximum(top, _fold_rows(sc, jnp.maximum)),
                n_pos + _fold_rows(jnp.where(sc > 0.0, 1.0, 0.0), jnp.add),
                n_nn + _fold_rows(jnp.where(sc >= 0.0, 1.0, 0.0), jnp.add))

    stats = (jnp.full((SUBLANES, tq), -jnp.inf, F32), jnp.zeros((SUBLANES, tq), F32),
             jnp.zeros((SUBLANES, tq), F32))
    stats = lax.fori_loop(0, n_full, functools.partial(score_body, masked=False), stats)
    stats = lax.fori_loop(n_full, n_chunks, functools.partial(score_body, masked=True), stats)

    top_key = _f32_to_key(jnp.max(stats[0], axis=0, keepdims=True))
    c_pos = jnp.sum(stats[1], axis=0, keepdims=True)
    c_nn = jnp.sum(stats[2], axis=0, keepdims=True)
    pos_side, zero_tie = c_pos >= kf, c_nn >= kf
    n_tot = (n_chunks * ck).astype(F32)
    lo = jnp.where(pos_side, KEY_MIN_NORMAL, jnp.where(zero_tie, 0, KEY_NEG_INF)).astype(I32)
    hi = jnp.where(pos_side, KEY_POS_INF, jnp.where(zero_tie, 1, 0)).astype(I32)
    c_lo = jnp.where(pos_side, c_pos, jnp.where(zero_tie, c_nn, n_tot))
    c_hi = jnp.where(pos_side, 0.0, jnp.where(zero_tie, c_pos, c_nn))

    def unsettled(lo, hi, c_lo):
        return (c_lo != kf) & (hi > lo + 1)

    def probe(state):
        it, lo, hi, c_lo, c_hi = state
        mid = (lo >> 1) + (hi >> 1) + (lo & hi & 1)
        near_top = jnp.where(hi > 0, jnp.maximum(mid, hi - PROBE_STEP), mid)
        cand = jnp.where(it == 0, top_key, jnp.where(it == 1, near_top, mid))
        cand = jnp.minimum(jnp.maximum(cand, lo + 1), hi - 1)
        cand_f = _key_to_f32(cand)
        c = count(lambda s: s >= cand_f)
        live = unsettled(lo, hi, c_lo)
        up = live & (c >= kf)
        down = live & (c < kf)
        return (it + 1, jnp.where(up, cand, lo), jnp.where(down, cand, hi),
                jnp.where(up, c, c_lo), jnp.where(down, c, c_hi))

    def searching(state):
        it, lo, hi, c_lo, _ = state
        left = jnp.max(jnp.where(unsettled(lo, hi, c_lo), 1.0, 0.0))
        return (it < MAX_PROBES) & (left > 0.0)

    def probes(state):
        for _ in range(PROBES_PER_CHECK):
            state = probe(state)
        return state

    _, lo, hi, _, c_hi = lax.while_loop(searching, probes, (jnp.int32(0), lo, hi, c_lo, c_hi))
    thr = _key_to_f32(lo)

    need = jnp.where(hi == lo + 1, kf - c_hi, float(ck) * n_tot)
    r_i = lax.broadcasted_iota(I32, (ck, ck), 0)
    c_i = lax.broadcasted_iota(I32, (ck, ck), 1)
    lower = jnp.where(c_i <= r_i, 1.0, 0.0).astype(BF16)

    def bias_body(j, seen, masked):
        start = pl.multiple_of(j * ck, ck)
        groups = range(0, ck, COUNT_ROWS)
        ties = jnp.concatenate(
            [jnp.where(score_buf[j, r0:r0 + COUNT_ROWS, :] == thr, 1.0, 0.0).astype(BF16)
             for r0 in groups], axis=0)
        rank = _dot(lower, ties) + seen
        for r0 in groups:
            rows = slice(r0, r0 + COUNT_ROWS)
            sc = score_buf[j, rows, :]
            keep = jnp.where(sc > thr, 0.0, jnp.where(
                sc == thr, jnp.where(rank[rows] <= need, 0.0, NEG_INF), NEG_INF))
            if masked:
                keep = jnp.where(start + kiota[rows] <= qpos, keep, NEG_INF)
            bias_buf[j, rows, :] = keep
        return rank[ck - 1:ck, :]

    seen = lax.fori_loop(0, n_full, functools.partial(bias_body, masked=False),
                         jnp.zeros((1, tq), F32))
    lax.fori_loop(n_full, n_chunks, functools.partial(bias_body, masked=True), seen)

    qs = [dq_ref[0, h] for h in range(N_HEADS)]

    def scores(j):
        start = pl.multiple_of(j * ck, ck)
        kc = dk_ref[0, pl.ds(start, ck), :]
        bias = bias_buf[j]
        return [_dot_nt(kc, qs[h]) + bias for h in range(N_HEADS)]

    carry = _attention_loops(n_chunks, n_chunks, tq, scores, lambda j, s: s, None,
                             lambda j, h: dvt_ref[0, j])
    for h in range(N_HEADS):
        _, l, acc = carry[h]
        o_ref[0, h * HEAD_DIM:(h + 1) * HEAD_DIM, :] = (acc / l).astype(BF16)


def _dsa_attention(q_aug, iq, iw_t, ik, k_aug, dv, tq, ck):
    b, nh, s, dh = iq.shape
    topk = min(DSA_TOPK, s // 4)
    assert ck >= topk, "every query must see at least topk (possibly masked) candidates"
    n_ck = s // ck
    v_t = jnp.swapaxes(dv.reshape(b, n_ck, ck, dh), 2, 3)
    return pl.pallas_call(
        functools.partial(_dsa_kernel, tq=tq, ck=ck, topk=topk),
        grid=(b, s // tq),
        in_specs=[pl.BlockSpec((1, nh, tq, AUG), lambda bi, i: (bi, 0, i, 0)),
                  pl.BlockSpec((1, nh, tq, dh), lambda bi, i: (bi, 0, i, 0)),
                  pl.BlockSpec((1, 8, tq), lambda bi, i: (bi, 0, i)),
                  pl.BlockSpec((1, s, dh), lambda bi, i: (bi, 0, 0)),
                  pl.BlockSpec((1, s, AUG), lambda bi, i: (bi, 0, 0)),
                  pl.BlockSpec((1, n_ck, dh, ck), lambda bi, i: (bi, 0, 0, 0))],
        out_specs=pl.BlockSpec((1, nh * dh, tq), lambda bi, i: (bi, 0, i)),
        out_shape=jax.ShapeDtypeStruct((b, nh * dh, s), BF16),
        scratch_shapes=[pltpu.VMEM((n_ck, ck, tq), F32), pltpu.VMEM((n_ck, ck, tq), F32)],
        compiler_params=_cparams("parallel", "arbitrary"),
        name="dsa_attention",
    )(q_aug, iq, iw_t, ik, k_aug, v_t)


def _merge_kernel(h_ref, a_ref, b_ref, c_ref, d_ref, wg_ref, bg_ref, wb_ref, wo_ref, g_ref,
                  beta_ref, o_ref, *, alpha):
    h = h_ref[...]
    hb = h.astype(BF16)
    d = h.shape[1]
    acc = jnp.zeros(h.shape, F32)
    for n, br in enumerate((a_ref, b_ref, c_ref, d_ref)):
        gate = _sigmoid(_dot(hb, wg_ref[:, n * d:(n + 1) * d]) + bg_ref[:, n * d:(n + 1) * d])
        acc = acc + gate * _dot(br[...], wb_ref[n])
    mixed = _dot(acc.astype(BF16), wo_ref[...])
    o_ref[...] = _ln_rows(alpha * h + mixed, g_ref[...], beta_ref[...])


def _merge(h2d, branches, w_gate, b_gate, w_branch, w_out, ln_g, ln_b, alpha, tm):
    n, d = h2d.shape
    c = branches[0].shape[1]
    rows = lambda w: pl.BlockSpec((tm, w), lambda i: (i, 0))
    const = lambda shp: pl.BlockSpec(shp, lambda i: (0,) * len(shp))
    return pl.pallas_call(
        functools.partial(_merge_kernel, alpha=alpha),
        grid=(n // tm,),
        in_specs=[rows(d), rows(c), rows(c), rows(c), rows(c), const((d, N_BRANCH * d)),
                  const((1, N_BRANCH * d)), const((N_BRANCH, c, d)), const((d, d)),
                  const((1, d)), const((1, d))],
        out_specs=rows(d),
        out_shape=jax.ShapeDtypeStruct((n, d), F32),
        compiler_params=_cparams("parallel"),
        name="merge",
    )(h2d, *branches, w_gate, b_gate.reshape(1, -1), w_branch, w_out, ln_g.reshape(1, d),
      ln_b.reshape(1, d))


ROUTER_E0 = N_GROUPS
MOE_BLK = 256


def _routing_t(logits):
    row = lax.broadcasted_iota(I32, logits.shape, 0)
    far = jnp.int32(4 * LANES)
    neg = -jnp.inf

    def first_max(x):
        mx = jnp.max(x, axis=0, keepdims=True)
        idx = jnp.min(jnp.where(x == mx, row, far), axis=0, keepdims=True)
        return mx, idx

    lg = jnp.where(row < N_GROUPS, logits, neg)
    g_max, g_idx = first_max(lg)
    p_group = 1.0 / jnp.sum(jnp.exp(lg - g_max), axis=0, keepdims=True)
    lo = ROUTER_E0 + EXPERTS_PER_GROUP * g_idx
    le = jnp.where((row >= lo) & (row < lo + EXPERTS_PER_GROUP), logits, neg)
    m1, e1 = first_max(le)
    m2, e2 = first_max(jnp.where(row == e1, neg, le))
    r = jnp.exp(m2 - m1)
    gate1 = p_group * (1.0 / (1.0 + r))
    gate2 = p_group * (r / (1.0 + r))
    return jnp.where(row == e1, gate1, jnp.where(row == e2, gate2, 0.0)), g_idx


def _bf16_pieces(x, n):
    out, rest = [], x
    for _ in range(n):
        part = rest.astype(BF16)
        out.append(part)
        rest = rest - part.astype(F32)
    return out


def _moe_kernel(h_ref, wr_ref, br_ref, wg_ref, wu_ref, wd_ref, g_ref, beta_ref, o_ref, blk_ref, p_ref,
                xs_ref, gs_ref, acc_ref, *, alpha):
    e = pl.program_id(1)
    t, d = h_ref.shape
    f = wd_ref.shape[2]
    n_slots = p_ref.shape[0] // MOE_BLK

    @pl.when(e == 0)
    def _():
        h = h_ref[...]
        hb = h.astype(BF16)
        gates, g_idx = _routing_t(_dot_nt(wr_ref[...], hb) + br_ref[...])
        grp = lax.broadcasted_iota(I32, (SUBLANES, t), 0)
        onehot = jnp.where(grp == g_idx, 1.0, 0.0)
        r_i = lax.broadcasted_iota(I32, (t, t), 0)
        c_i = lax.broadcasted_iota(I32, (t, t), 1)
        before = jnp.where(r_i < c_i, 1.0, 0.0).astype(BF16)
        rank = _dot(onehot.astype(BF16), before)
        total = jnp.sum(onehot, axis=1, keepdims=True)
        start = jnp.int32(0)
        starts = jnp.zeros((SUBLANES, 1), F32)
        grp_col = lax.broadcasted_iota(I32, (SUBLANES, 1), 0)
        for g in range(N_GROUPS):
            n_g = (total[g, 0].astype(I32) + (MOE_BLK - 1)) // MOE_BLK
            blk_ref[2 * g] = start
            blk_ref[2 * g + 1] = n_g
            starts = jnp.where(grp_col == g, (start * MOE_BLK).astype(F32), starts)
            start = start + n_g
        blk_ref[2 * N_GROUPS] = start
        pos = jnp.sum(onehot * (starts + rank), axis=0, keepdims=True).astype(I32)
        slot = lax.broadcasted_iota(I32, (p_ref.shape[0], t), 0)
        p_ref[...] = jnp.where(slot == pos, 1.0, 0.0).astype(BF16)
        perm = p_ref[...]
        xs = _dot_nt(h.T.astype(BF16), perm).astype(BF16)
        gs = sum(_dot_nt(piece, perm) for piece in _bf16_pieces(gates, 3))
        for b in range(n_slots):
            cols = slice(b * MOE_BLK, (b + 1) * MOE_BLK)
            xs_ref[b] = xs[:, cols]
            acc_ref[b] = jnp.zeros((d, MOE_BLK), F32)
            for k in range(N_EXPERTS):
                gs_ref[b, k] = gs[ROUTER_E0 + k:ROUTER_E0 + k + 1, cols]

    grp_e = e // EXPERTS_PER_GROUP
    blk0 = blk_ref[2 * grp_e]

    def expert_block(i, _):
        b = blk0 + i
        xs = xs_ref[b]
        hg = _dot(wg_ref[0], xs)
        hu = _dot(wu_ref[0], xs)
        hid = (hg * _sigmoid(hg) * hu).astype(BF16)
        y = jnp.concatenate([_dot(wd_ref[0, :d // 2], hid), _dot(wd_ref[0, d // 2:], hid)], axis=0)
        acc_ref[b] += gs_ref[b, e] * y
        return 0

    lax.fori_loop(0, blk_ref[2 * grp_e + 1], expert_block, 0)

    @pl.when(e == pl.num_programs(1) - 1)
    def _():
        pieces = [jnp.concatenate(col, axis=1) for col in
                  zip(*[_bf16_pieces(acc_ref[b], 2) for b in range(n_slots)])]
        half = t // 2
        for c in range(2):
            cols = slice(c * half, (c + 1) * half)
            out_t = sum(_dot(piece, p_ref[:, cols]) for piece in pieces)
            o_ref[cols, :] = _ln_rows(alpha * h_ref[cols, :] + out_t.T, g_ref[...], beta_ref[...])


def _moe(h2d, w_router_t, b_router_t, wg_t, wu_t, wd_t, ln_g, ln_b, alpha, tm):
    n, d = h2d.shape
    ne, _, f = wd_t.shape
    assert tm % MOE_BLK == 0
    n_slots = tm // MOE_BLK + N_GROUPS - 1
    rows = pl.BlockSpec((tm, d), lambda i, e: (i, 0))
    const = lambda shp: pl.BlockSpec(shp, lambda i, e: (0,) * len(shp))
    return pl.pallas_call(
        functools.partial(_moe_kernel, alpha=alpha),
        grid=(n // tm, ne),
        in_specs=[rows, const((LANES, d)), const((LANES, 1)),
                  pl.BlockSpec((1, f, d), lambda i, e: (e, 0, 0)),
                  pl.BlockSpec((1, f, d), lambda i, e: (e, 0, 0)),
                  pl.BlockSpec((1, d, f), lambda i, e: (e, 0, 0)),
                  const((1, d)), const((1, d))],
        out_specs=rows,
        out_shape=jax.ShapeDtypeStruct((n, d), F32),
        scratch_shapes=[pltpu.SMEM((2 * N_GROUPS + 1,), I32),
                        pltpu.VMEM((n_slots * MOE_BLK, tm), BF16),
                        pltpu.VMEM((n_slots, d, MOE_BLK), BF16),
                        pltpu.VMEM((n_slots, ne, 1, MOE_BLK), F32),
                        pltpu.VMEM((n_slots, d, MOE_BLK), F32)],
        compiler_params=pltpu.CompilerParams(dimension_semantics=("parallel", "arbitrary"),
                                             vmem_limit_bytes=MOE_VMEM_LIMIT),
        name="moe",
    )(h2d, w_router_t, b_router_t, wg_t, wu_t, wd_t, ln_g.reshape(1, d), ln_b.reshape(1, d))


def _tiles(b, s):
    n = b * s
    pick = lambda total, want: want if total % want == 0 else total
    return dict(
        ln_tm=pick(n, 512), proj_tm=pick(s, 512), local_ts=pick(s, 512),
        fox_tq=pick(s, 256), fox_ck=pick(s, 512), dsa_tq=pick(s, 256), dsa_ck=pick(s, 512),
        merge_tm=pick(n, 1024), moe_tm=pick(n, 1024))


def kernel(x, ln_in_g, ln_in_b, w_in, b_forget, b_gate, pool_w, pool_scale, conv_dw, conv_b,
           conv_ln_g, conv_ln_b, w_branch, w_out, ln1_g, ln1_b, router_g, router_g_b, router_e,
           router_e_b, expert_w_gate, expert_w_up, expert_w_down, ln2_g, ln2_b):
    b, s, d = x.shape
    depth = w_in.shape[0]
    alpha = (2.0 * depth) ** 0.25
    t = _tiles(b, s)
    n = b * s

    h = _layer_norm(x.reshape(n, d), ln_in_g, ln_in_b, t["ln_tm"])
    for l in range(depth):
        wp = _pack_proj_weight(w_in[l])
        (pool_v, fq, fk, fv, dq, iq, glu, dk, dv, ik, small) = _projection(
            h.reshape(b, s, d), wp, t["proj_tm"])

        small_t = jnp.swapaxes(small[..., SMALL_FF:SMALL_FF + 2 * N_HEADS], 1, 2)
        f_t = jnp.pad(small_t[:, :N_HEADS], ((0, 0), (0, 8 - N_HEADS), (0, 0)))
        iw_t = jnp.pad(small_t[:, N_HEADS:], ((0, 0), (0, 8 - N_HEADS), (0, 0)))
        bf_col = jnp.pad(b_forget[l], (0, 8 - N_HEADS)).reshape(8, 1)
        c_parts = _forget_cumsum(f_t, bf_col)[:, :, :N_HEADS]

        pool_wbd = jnp.zeros((BRANCH_W, BRANCH_W), F32)
        for g in range(len(POOL_WINDOWS)):
            sl = slice(g * POOL_GROUP, (g + 1) * POOL_GROUP)
            pool_wbd = pool_wbd.at[sl, sl].set(pool_w[l, g])
        br_a, br_d = _local_mixers(pool_v, glu, pool_wbd.astype(BF16), pool_scale[l], conv_dw[l],
                                   conv_b[l], conv_ln_g[l], conv_ln_b[l], t["local_ts"])
        br_b = _fox_attention(fq, fk, fv, c_parts, t["fox_tq"], t["fox_ck"])
        br_c = _dsa_attention(dq, iq, iw_t, ik, dk, dv, t["dsa_tq"], t["dsa_ck"])
        br_b, br_c = jnp.swapaxes(br_b, 1, 2), jnp.swapaxes(br_c, 1, 2)

        branches = [br.reshape(n, BRANCH_W) for br in (br_a, br_b, br_c, br_d)]
        h = _merge(h, branches, w_in[l][:, GATE_OFF:].astype(BF16), b_gate[l],
                   w_branch[l].astype(BF16), w_out[l].astype(BF16), ln1_g[l], ln1_b[l], alpha,
                   t["merge_tm"])

        w_router_t = jnp.concatenate(
            [router_g[l], router_e[l], jnp.zeros((d, LANES - N_GROUPS - N_EXPERTS), F32)],
            axis=1).T.astype(BF16)
        b_router_t = jnp.concatenate(
            [router_g_b[l], router_e_b[l], jnp.zeros((LANES - N_GROUPS - N_EXPERTS,), F32)]
        ).reshape(LANES, 1)
        wg_t = jnp.swapaxes(expert_w_gate[l].astype(BF16), 1, 2)
        wu_t = jnp.swapaxes(expert_w_up[l].astype(BF16), 1, 2)
        wd_t = jnp.swapaxes(expert_w_down[l].astype(BF16), 1, 2)
        h = _moe(h, w_router_t, b_router_t, wg_t, wu_t, wd_t, ln2_g[l], ln2_b[l], alpha,
                 t["moe_tm"])
    return h.reshape(b, s, d)
```

```python
import functools

import jax
import jax.numpy as jnp
from jax import lax
from jax.experimental import pallas as pl
from jax.experimental.pallas import tpu as pltpu

F32 = jnp.float32
BF16 = jnp.bfloat16
I32 = jnp.int32

N_BRANCH = 4
HEAD_DIM = 64
N_HEADS = 4
BRANCH_W = N_HEADS * HEAD_DIM
POOL_WINDOWS = (2, 4, 8, 16)
POOL_GROUP = BRANCH_W // len(POOL_WINDOWS)
CONV_W = 31
DSA_TOPK = 256
N_GROUPS = 4
EXPERTS_PER_GROUP = 4
N_EXPERTS = N_GROUPS * EXPERTS_PER_GROUP
LN_EPS = 1e-5
NEG_INF = -1e30
ATTN_SCALE = HEAD_DIM ** -0.5

LANES = 128
HALO = 32
VMEM_LIMIT = 56 * 1024 * 1024
MOE_VMEM_LIMIT = 60 * 1024 * 1024

_OFF = {}
_o = 0
for _name, _n in (("pool", BRANCH_W), ("fq", BRANCH_W), ("fk", BRANCH_W), ("fv", BRANCH_W),
                  ("ff", N_HEADS), ("dq", BRANCH_W), ("dk", HEAD_DIM), ("dv", HEAD_DIM),
                  ("iq", BRANCH_W), ("ik", HEAD_DIM), ("iw", N_HEADS), ("conv", 2 * BRANCH_W)):
    _OFF[_name] = (_o, _o + _n)
    _o += _n
GATE_OFF = _o
SMALL_FF = HEAD_DIM
SMALL_IW = HEAD_DIM + N_HEADS
AUG = 2 * HEAD_DIM
POS_DIGIT_BITS = 6


def _cparams(*sem):
    return pltpu.CompilerParams(dimension_semantics=sem, vmem_limit_bytes=VMEM_LIMIT)


def _ln_rows(x, g, b):
    mu = jnp.mean(x, axis=-1, keepdims=True)
    xc = x - mu
    var = jnp.mean(xc * xc, axis=-1, keepdims=True)
    return xc * lax.rsqrt(var + LN_EPS) * g + b


def _sigmoid(x):
    return 1.0 / (1.0 + jnp.exp(-x))


def _dot(a, b):
    return jnp.dot(a, b, preferred_element_type=F32)


def _dot_nt(a, b):
    return lax.dot_general(a, b, (((1,), (1,)), ((), ())), preferred_element_type=F32)


def _ln_kernel(x_ref, g_ref, b_ref, o_ref):
    o_ref[...] = _ln_rows(x_ref[...], g_ref[...], b_ref[...])


def _layer_norm(x2d, g, b, tm):
    n, d = x2d.shape
    return pl.pallas_call(
        _ln_kernel,
        grid=(n // tm,),
        in_specs=[pl.BlockSpec((tm, d), lambda i: (i, 0)),
                  pl.BlockSpec((1, d), lambda i: (0, 0)),
                  pl.BlockSpec((1, d), lambda i: (0, 0))],
        out_specs=pl.BlockSpec((tm, d), lambda i: (i, 0)),
        out_shape=jax.ShapeDtypeStruct((n, d), F32),
        compiler_params=_cparams("parallel"),
        name="ln_in",
    )(x2d, g.reshape(1, d), b.reshape(1, d))


_PK = {"pool": 0, "fq": 256, "fk": 512, "fv": 768, "dq": 1024, "iq": 1280, "conv": 1536,
       "dkv": 2048, "small": 2176}
PK_COLS = 2304


def _pack_proj_weight(w):
    d = w.shape[0]
    sl = lambda k: w[:, _OFF[k][0]:_OFF[k][1]]
    small = jnp.concatenate([sl("ik"), sl("ff"), sl("iw"),
                             jnp.zeros((d, LANES - HEAD_DIM - 2 * N_HEADS), w.dtype)], axis=1)
    packed = jnp.concatenate([sl("pool"), sl("fq") * ATTN_SCALE, sl("fk"), sl("fv"),
                              sl("dq") * ATTN_SCALE, sl("iq"), sl("conv"), sl("dk"), sl("dv"),
                              small], axis=1)
    return packed.astype(BF16)


def _proj_kernel(h_ref, w_ref, pool_o, fq_o, fk_o, fv_o, dq_o, iq_o, glu_o, dk_o, dv_o, ik_o,
                 small_o):
    hb = h_ref[0].astype(BF16)
    tm = hb.shape[0]

    def piece(name, width):
        return _dot(hb, w_ref[:, _PK[name]:_PK[name] + width])

    lane = lax.broadcasted_iota(I32, (tm, HEAD_DIM), 1)
    pos = pl.program_id(1) * tm + lax.broadcasted_iota(I32, (tm, HEAD_DIM), 0)
    p_hi = (pos >> POS_DIGIT_BITS).astype(F32)
    p_lo = (pos & ((1 << POS_DIGIT_BITS) - 1)).astype(F32)
    digit = float(1 << POS_DIGIT_BITS)

    def bias_cols(c0, c1, c2, c3):
        cols = jnp.where(lane == 0, c0, jnp.where(lane == 1, c1, jnp.where(
            lane == 2, c2, jnp.where(lane == 3, c3, 0.0))))
        return cols.astype(BF16)

    def widen(r, cols):
        return jnp.concatenate([r, cols], axis=1)

    pool_o[0] = piece("pool", BRANCH_W)
    for name, out in (("fq", fq_o), ("fk", fk_o), ("fv", fv_o), ("dq", dq_o), ("iq", iq_o)):
        r = piece(name, BRANCH_W).astype(BF16)
        for h in range(N_HEADS):
            r_h = r[:, h * HEAD_DIM:(h + 1) * HEAD_DIM]
            if name == "fq":
                r_h = widen(r_h, bias_cols(-1.0, -1.0, -1.0, 0.0))
            elif name == "dq":
                slope = 2.0 ** (-8.0 * (h + 1) / N_HEADS)
                r_h = widen(r_h, bias_cols(digit * slope, slope, -digit * slope * p_hi,
                                           -slope * p_lo))
            out[0, h] = r_h
    a = piece("conv", 2 * BRANCH_W)
    glu_o[0] = a[:, :BRANCH_W] * _sigmoid(a[:, BRANCH_W:])
    kv = piece("dkv", 2 * HEAD_DIM).astype(BF16)
    dk_o[0] = widen(kv[:, :HEAD_DIM], bias_cols(p_hi, p_lo, 1.0, 1.0))
    dv_o[0] = kv[:, HEAD_DIM:]
    sm = piece("small", LANES)
    small_o[0] = sm
    ik_o[0] = sm[:, :HEAD_DIM].astype(BF16)


def _projection(h, wp, tm):
    b, s, d = h.shape
    assert s <= (1 << POS_DIGIT_BITS) ** 2, "positions are carried as two digits"
    heads = lambda w: pl.BlockSpec((1, N_HEADS, tm, w), lambda bi, i: (bi, 0, i, 0))
    rows = lambda w: pl.BlockSpec((1, tm, w), lambda bi, i: (bi, i, 0))
    head_shape = lambda w: jax.ShapeDtypeStruct((b, N_HEADS, s, w), BF16)
    hd, aug = HEAD_DIM, AUG
    return pl.pallas_call(
        _proj_kernel,
        grid=(b, s // tm),
        in_specs=[rows(d), pl.BlockSpec((d, PK_COLS), lambda bi, i: (0, 0))],
        out_specs=[rows(BRANCH_W), heads(aug), heads(hd), heads(hd), heads(aug), heads(hd),
                   rows(BRANCH_W), rows(aug), rows(hd), rows(hd), rows(LANES)],
        out_shape=[jax.ShapeDtypeStruct((b, s, BRANCH_W), F32), head_shape(aug), head_shape(hd),
                   head_shape(hd), head_shape(aug), head_shape(hd),
                   jax.ShapeDtypeStruct((b, s, BRANCH_W), F32),
                   jax.ShapeDtypeStruct((b, s, AUG), BF16),
                   jax.ShapeDtypeStruct((b, s, HEAD_DIM), BF16),
                   jax.ShapeDtypeStruct((b, s, HEAD_DIM), BF16),
                   jax.ShapeDtypeStruct((b, s, LANES), F32)],
        compiler_params=_cparams("parallel", "parallel"),
        name="proj",
    )(h, wp)


def _forget_cumsum_kernel(f_ref, b_ref, o_ref):
    s = f_ref.shape[2]
    x = f_ref[0] + b_ref[...]
    ls = jnp.minimum(x, 0.0) - jnp.log(1.0 + jnp.exp(-jnp.abs(x)))
    r = lax.broadcasted_iota(I32, (LANES, LANES), 0)
    c = lax.broadcasted_iota(I32, (LANES, LANES), 1)
    upper = jnp.where(r <= c, 1.0, 0.0).astype(F32)
    carry = jnp.zeros((x.shape[0], 1), F32)
    for j in range(s // LANES):
        blk = jnp.dot(ls[:, j * LANES:(j + 1) * LANES], upper, preferred_element_type=F32,
                      precision=lax.Precision.HIGHEST) + carry
        carry = blk[:, LANES - 1:LANES]
        rest = blk
        for piece in range(3):
            part = rest.astype(BF16)
            o_ref[0, piece, :, j * LANES:(j + 1) * LANES] = part
            rest = rest - part.astype(F32)


def _forget_cumsum(f_t, b_col):
    b, r, s = f_t.shape
    return pl.pallas_call(
        _forget_cumsum_kernel,
        grid=(b,),
        in_specs=[pl.BlockSpec((1, r, s), lambda bi: (bi, 0, 0)),
                  pl.BlockSpec((r, 1), lambda bi: (0, 0))],
        out_specs=pl.BlockSpec((1, 3, r, s), lambda bi: (bi, 0, 0, 0)),
        out_shape=jax.ShapeDtypeStruct((b, 3, r, s), BF16),
        compiler_params=_cparams("parallel"),
        name="forget_cumsum",
    )(f_t, b_col)


def _local_kernel(pv_ref, pvh_ref, gl_ref, glh_ref, pw_ref, ps_ref, dw_ref, cb_ref, g_ref, b_ref,
                  a_o, d_o, pbuf, cbuf, shifted, *, ts):
    i = pl.program_id(1)
    first = i == 0
    pbuf[0:HALO, :] = jnp.where(first, 0.0, pvh_ref[0])
    pbuf[HALO:HALO + ts, :] = pv_ref[0]
    cbuf[0:HALO, :] = jnp.where(first, 0.0, glh_ref[0])
    cbuf[HALO:HALO + ts, :] = gl_ref[0]

    def shifted_window(buf, back):
        whole, r = divmod(back, SUBLANES)
        lo = HALO - whole * SUBLANES
        return buf[lo:lo + ts, :] if r == 0 else shifted[r - 1, lo:lo + ts, :]

    def fill_shifted(buf):
        for r in range(1, SUBLANES):
            shifted[r - 1, SUBLANES:, :] = buf[SUBLANES - r:HALO + ts - r, :]

    fill_shifted(pbuf)
    lane = lax.broadcasted_iota(I32, (ts, BRANCH_W), 1)
    pos = (i * ts + 1 + lax.broadcasted_iota(I32, (ts, 1), 0)).astype(F32)
    v = pbuf[HALO:HALO + ts, :]
    run = v
    pooled = jnp.zeros((ts, BRANCH_W), F32)
    shift = 1
    for g, w in enumerate(POOL_WINDOWS):
        while shift < w:
            run = run + shifted_window(pbuf, shift)
            shift += 1
        in_group = (lane >= g * POOL_GROUP) & (lane < (g + 1) * POOL_GROUP)
        pooled = jnp.where(in_group, run / jnp.minimum(pos, float(w)), pooled)
    pooled = pooled - v
    a_o[0] = (_dot(pooled.astype(BF16), pw_ref[...]) * ps_ref[...]).astype(BF16)

    fill_shifted(cbuf)
    y = jnp.zeros((ts, BRANCH_W), F32) + cb_ref[...]
    for j in range(CONV_W):
        y = y + shifted_window(cbuf, CONV_W - 1 - j) * dw_ref[j:j + 1, :]
    z = _ln_rows(y, g_ref[...], b_ref[...])
    d_o[0] = (z * _sigmoid(z)).astype(BF16)


def _local_mixers(pool_v, glu, pool_wbd, pool_scale, conv_dw, conv_b, ln_g, ln_b, ts):
    b, s, c = pool_v.shape
    cur = pl.BlockSpec((1, ts, c), lambda bi, i: (bi, i, 0))
    halo = pl.BlockSpec((1, HALO, c), lambda bi, i: (bi, jnp.maximum(i * (ts // HALO) - 1, 0), 0))
    full2 = lambda shp: pl.BlockSpec(shp, lambda bi, i: (0, 0))
    out_shape = jax.ShapeDtypeStruct((b, s, c), BF16)
    return pl.pallas_call(
        functools.partial(_local_kernel, ts=ts),
        grid=(b, s // ts),
        in_specs=[cur, halo, cur, halo, full2((c, c)), full2((1, c)), full2((CONV_W, c)),
                  full2((1, c)), full2((1, c)), full2((1, c))],
        out_specs=[cur, cur],
        out_shape=[out_shape, out_shape],
        scratch_shapes=[pltpu.VMEM((HALO + ts, c), F32), pltpu.VMEM((HALO + ts, c), F32),
                        pltpu.VMEM((SUBLANES - 1, HALO + ts, c), F32)],
        compiler_params=_cparams("parallel", "parallel"),
        name="local_mixers",
    )(pool_v, pool_v, glu, glu, pool_wbd, pool_scale.reshape(1, c), conv_dw, conv_b.reshape(1, c),
      ln_g.reshape(1, c), ln_b.reshape(1, c))


SUBLANES = 8
FOLD_CHAINS = 8


def _fold_rows(x, op):
    slabs = [x[i:i + SUBLANES] for i in range(0, x.shape[0], SUBLANES)]
    accs = slabs[:FOLD_CHAINS]
    for i, slab in enumerate(slabs[FOLD_CHAINS:]):
        accs[i % FOLD_CHAINS] = op(accs[i % FOLD_CHAINS], slab)
    while len(accs) > 1:
        accs = [op(accs[i], accs[i + 1]) for i in range(0, len(accs) - 1, 2)] + (
            [accs[-1]] if len(accs) % 2 else [])
    return accs[0]


def _softmax_chunk(s, m, l):
    m_new = jnp.maximum(m, jnp.max(_fold_rows(s, jnp.maximum), axis=0, keepdims=True))
    alpha = jnp.exp(m - m_new)
    p = jnp.exp(s - m_new)
    l_new = alpha * l + jnp.sum(_fold_rows(p, jnp.add), axis=0, keepdims=True)
    return m_new, l_new, alpha, p.astype(BF16)


def _causal_split(qi, tq, ck):
    return (qi * tq + 1) // ck, ((qi + 1) * tq + ck - 1) // ck


CHUNK_STEPS = (4, 2, 1)


def _attention_loops(n_plain, n_chunks, tq, scores, adjust, edit_last, values):
    def consume(j, ss, carry, edit):
        stats, ps = [], []
        for h in range(N_HEADS):
            s = adjust(j, ss[h])
            if edit:
                s = edit_last(j, s)
            m_new, l_new, alpha, p = _softmax_chunk(s, carry[h][0], carry[h][1])
            stats.append((m_new, l_new, alpha))
            ps.append(p)
        out = []
        for h in range(N_HEADS):
            m_new, l_new, alpha = stats[h]
            out.append((m_new, l_new, alpha * carry[h][2] + _dot(values(j, h), ps[h])))
        return out

    def step(j0, carry, n, edit):
        ss = [scores(j0 + c) for c in range(n)]
        carry = list(carry)
        for c in range(n):
            carry = consume(j0 + c, ss[c], carry, edit)
        return tuple(carry)

    carry = tuple((jnp.full((1, tq), NEG_INF, F32), jnp.zeros((1, tq), F32),
                   jnp.zeros((HEAD_DIM, tq), F32)) for _ in range(N_HEADS))
    done = 0
    for width in CHUNK_STEPS:
        n_steps = (n_plain - done) // width
        carry = lax.fori_loop(
            0, n_steps, lambda i, c, w=width, d0=done: step(d0 + i * w, c, w, False), carry)
        done = done + n_steps * width
    if edit_last is None:
        return carry
    return lax.fori_loop(n_plain, n_chunks, lambda j, c: step(j, c, 1, True), carry)


def _fox_kernel(q_ref, k_ref, vt_ref, o_ref, *, tq, ck):
    qi = pl.program_id(1)
    n_full, n_chunks = _causal_split(qi, tq, ck)
    qpos = qi * tq + lax.broadcasted_iota(I32, (1, tq), 1)
    kiota = lax.broadcasted_iota(I32, (ck, 1), 0)
    qs = [q_ref[0, h] for h in range(N_HEADS)]

    def scores(j):
        start = pl.multiple_of(j * ck, ck)
        return [_dot_nt(k_ref[0, h, pl.ds(start, ck), :], qs[h]) for h in range(N_HEADS)]

    def causal(j, s):
        return jnp.where(j * ck + kiota <= qpos, s, NEG_INF)

    def values(j, h):
        return vt_ref[0, h, j]

    carry = _attention_loops(n_full, n_chunks, tq, scores, lambda j, s: s, causal, values)
    for h in range(N_HEADS):
        _, l, acc = carry[h]
        o_ref[0, :, h * HEAD_DIM:(h + 1) * HEAD_DIM] = (acc / l).T.astype(BF16)


def _fox_attention(q_aug, fk, fv, c_parts, tq, ck):
    b, nh, s, dh = fk.shape
    n_ck = s // ck
    pad = jnp.zeros((b, nh, s, AUG - dh - 3), BF16)
    k_aug = jnp.concatenate([fk, jnp.moveaxis(c_parts, 1, 3), pad], axis=-1)
    v_t = jnp.swapaxes(fv.reshape(b, nh, n_ck, ck, dh), 3, 4)
    return pl.pallas_call(
        functools.partial(_fox_kernel, tq=tq, ck=ck),
        grid=(b, s // tq),
        in_specs=[pl.BlockSpec((1, nh, tq, AUG), lambda bi, i: (bi, 0, i, 0)),
                  pl.BlockSpec((1, nh, s, AUG), lambda bi, i: (bi, 0, 0, 0)),
                  pl.BlockSpec((1, nh, n_ck, dh, ck), lambda bi, i: (bi, 0, 0, 0, 0))],
        out_specs=pl.BlockSpec((1, tq, nh * dh), lambda bi, i: (bi, i, 0)),
        out_shape=jax.ShapeDtypeStruct((b, s, nh * dh), BF16),
        compiler_params=_cparams("parallel", "arbitrary"),
        name="fox_attention",
    )(q_aug, k_aug, v_t)


def _key_to_f32(key):
    bits = key ^ (lax.shift_right_arithmetic(key, 31) & 0x7FFFFFFF)
    return lax.bitcast_convert_type(bits, F32)


def _f32_to_key(x):
    bits = lax.bitcast_convert_type(x, I32)
    return bits ^ (lax.shift_right_arithmetic(bits, 31) & 0x7FFFFFFF)


KEY_POS_INF = 0x7F800000
KEY_MIN_NORMAL = 0x00800000
KEY_NEG_INF = 0x007FFFFF - 2 ** 31
PROBE_STEP = 1 << 24
MAX_PROBES = 36
PROBES_PER_CHECK = 3
COUNT_ROWS = 64


def _dsa_kernel(dq_ref, iq_ref, iwt_ref, ik_ref, dk_ref, dvt_ref, o_ref, score_buf, bias_buf, *,
                tq, ck, topk):
    qi = pl.program_id(1)
    n_full, n_chunks = _causal_split(qi, tq, ck)
    qpos = qi * tq + lax.broadcasted_iota(I32, (1, tq), 1)
    kiota = lax.broadcasted_iota(I32, (ck, 1), 0)
    w_idx = iwt_ref[0]
    kf = float(topk)

    def count(pred):
        def cb(j, acc):
            for r0 in range(0, ck, COUNT_ROWS):
                blk = score_buf[j, r0:r0 + COUNT_ROWS, :]
                acc = acc + _fold_rows(jnp.where(pred(blk), 1.0, 0.0), jnp.add)
            return acc
        acc = lax.fori_loop(0, n_chunks, cb, jnp.zeros((SUBLANES, tq), F32))
        return jnp.sum(acc, axis=0, keepdims=True)

    def score_body(j, stats, masked):
        top, n_pos, n_nn = stats
        start = pl.multiple_of(j * ck, ck)
        ikc = ik_ref[0, pl.ds(start, ck), :]
        sc = None
        for h in range(N_HEADS):
            term = jnp.maximum(_dot_nt(ikc, iq_ref[0, h]), 0.0) * w_idx[h:h + 1, :]
            sc = term if sc is None else sc + term
        if masked:
            sc = jnp.where(start + kiota <= qpos, sc, NEG_INF)
        score_buf[j] = sc
        return (jnp.maximum(top, _fold_rows(sc, jnp.maximum)),
                n_pos + _fold_rows(jnp.where(sc > 0.0, 1.0, 0.0), jnp.add),
                n_nn + _fold_rows(jnp.where(sc >= 0.0, 1.0, 0.0), jnp.add))

    stats = (jnp.full((SUBLANES, tq), -jnp.inf, F32), jnp.zeros((SUBLANES, tq), F32),
             jnp.zeros((SUBLANES, tq), F32))
    stats = lax.fori_loop(0, n_full, functools.partial(score_body, masked=False), stats)
    stats = lax.fori_loop(n_full, n_chunks, functools.partial(score_body, masked=True), stats)

    top_key = _f32_to_key(jnp.max(stats[0], axis=0, keepdims=True))
    c_pos = jnp.sum(stats[1], axis=0, keepdims=True)
    c_nn = jnp.sum(stats[2], axis=0, keepdims=True)
    pos_side, zero_tie = c_pos >= kf, c_nn >= kf
    n_tot = (n_chunks * ck).astype(F32)
    lo = jnp.where(pos_side, KEY_MIN_NORMAL, jnp.where(zero_tie, 0, KEY_NEG_INF)).astype(I32)
    hi = jnp.where(pos_side, KEY_POS_INF, jnp.where(zero_tie, 1, 0)).astype(I32)
    c_lo = jnp.where(pos_side, c_pos, jnp.where(zero_tie, c_nn, n_tot))
    c_hi = jnp.where(pos_side, 0.0, jnp.where(zero_tie, c_pos, c_nn))

    def unsettled(lo, hi, c_lo):
        return (c_lo != kf) & (hi > lo + 1)

    def probe(state):
        it, lo, hi, c_lo, c_hi = state
        mid = (lo >> 1) + (hi >> 1) + (lo & hi & 1)
        near_top = jnp.where(hi > 0, jnp.maximum(mid, hi - PROBE_STEP), mid)
        cand = jnp.where(it == 0, top_key, jnp.where(it == 1, near_top, mid))
        cand = jnp.minimum(jnp.maximum(cand, lo + 1), hi - 1)
        cand_f = _key_to_f32(cand)
        c = count(lambda s: s >= cand_f)
        live = unsettled(lo, hi, c_lo)
        up = live & (c >= kf)
        down = live & (c < kf)
        return (it + 1, jnp.where(up, cand, lo), jnp.where(down, cand, hi),
                jnp.where(up, c, c_lo), jnp.where(down, c, c_hi))

    def searching(state):
        it, lo, hi, c_lo, _ = state
        left = jnp.max(jnp.where(unsettled(lo, hi, c_lo), 1.0, 0.0))
        return (it < MAX_PROBES) & (left > 0.0)

    def probes(state):
        for _ in range(PROBES_PER_CHECK):
            state = probe(state)
        return state

    _, lo, hi, _, c_hi = lax.while_loop(searching, probes, (jnp.int32(0), lo, hi, c_lo, c_hi))
    thr = _key_to_f32(lo)

    need = jnp.where(hi == lo + 1, kf - c_hi, float(ck) * n_tot)
    r_i = lax.broadcasted_iota(I32, (ck, ck), 0)
    c_i = lax.broadcasted_iota(I32, (ck, ck), 1)
    lower = jnp.where(c_i <= r_i, 1.0, 0.0).astype(BF16)

    def bias_body(j, seen, masked):
        start = pl.multiple_of(j * ck, ck)
        groups = range(0, ck, COUNT_ROWS)
        ties = jnp.concatenate(
            [jnp.where(score_buf[j, r0:r0 + COUNT_ROWS, :] == thr, 1.0, 0.0).astype(BF16)
             for r0 in groups], axis=0)
        rank = _dot(lower, ties) + seen
        for r0 in groups:
            rows = slice(r0, r0 + COUNT_ROWS)
            sc = score_buf[j, rows, :]
            keep = jnp.where(sc > thr, 0.0, jnp.where(
                sc == thr, jnp.where(rank[rows] <= need, 0.0, NEG_INF), NEG_INF))
            if masked:
                keep = jnp.where(start + kiota[rows] <= qpos, keep, NEG_INF)
            bias_buf[j, rows, :] = keep
        return rank[ck - 1:ck, :]

    seen = lax.fori_loop(0, n_full, functools.partial(bias_body, masked=False),
                         jnp.zeros((1, tq), F32))
    lax.fori_loop(n_full, n_chunks, functools.partial(bias_body, masked=True), seen)

    qs = [dq_ref[0, h] for h in range(N_HEADS)]

    def scores(j):
        start = pl.multiple_of(j * ck, ck)
        kc = dk_ref[0, pl.ds(start, ck), :]
        bias = bias_buf[j]
        return [_dot_nt(kc, qs[h]) + bias for h in range(N_HEADS)]

    carry = _attention_loops(n_chunks, n_chunks, tq, scores, lambda j, s: s, None,
                             lambda j, h: dvt_ref[0, j])
    for h in range(N_HEADS):
        _, l, acc = carry[h]
        o_ref[0, :, h * HEAD_DIM:(h + 1) * HEAD_DIM] = (acc / l).T.astype(BF16)


def _dsa_attention(q_aug, iq, iw_t, ik, k_aug, dv, tq, ck):
    b, nh, s, dh = iq.shape
    topk = min(DSA_TOPK, s // 4)
    assert ck >= topk, "every query must see at least topk (possibly masked) candidates"
    n_ck = s // ck
    v_t = jnp.swapaxes(dv.reshape(b, n_ck, ck, dh), 2, 3)
    return pl.pallas_call(
        functools.partial(_dsa_kernel, tq=tq, ck=ck, topk=topk),
        grid=(b, s // tq),
        in_specs=[pl.BlockSpec((1, nh, tq, AUG), lambda bi, i: (bi, 0, i, 0)),
                  pl.BlockSpec((1, nh, tq, dh), lambda bi, i: (bi, 0, i, 0)),
                  pl.BlockSpec((1, 8, tq), lambda bi, i: (bi, 0, i)),
                  pl.BlockSpec((1, s, dh), lambda bi, i: (bi, 0, 0)),
                  pl.BlockSpec((1, s, AUG), lambda bi, i: (bi, 0, 0)),
                  pl.BlockSpec((1, n_ck, dh, ck), lambda bi, i: (bi, 0, 0, 0))],
        out_specs=pl.BlockSpec((1, tq, nh * dh), lambda bi, i: (bi, i, 0)),
        out_shape=jax.ShapeDtypeStruct((b, s, nh * dh), BF16),
        scratch_shapes=[pltpu.VMEM((n_ck, ck, tq), F32), pltpu.VMEM((n_ck, ck, tq), F32)],
        compiler_params=_cparams("parallel", "arbitrary"),
        name="dsa_attention",
    )(q_aug, iq, iw_t, ik, k_aug, v_t)


def _merge_kernel(h_ref, a_ref, b_ref, c_ref, d_ref, wg_ref, bg_ref, wb_ref, wo_ref, g_ref,
                  beta_ref, o_ref, *, alpha):
    h = h_ref[...]
    hb = h.astype(BF16)
    d = h.shape[1]
    acc = jnp.zeros(h.shape, F32)
    for n, br in enumerate((a_ref, b_ref, c_ref, d_ref)):
        gate = _sigmoid(_dot(hb, wg_ref[:, n * d:(n + 1) * d]) + bg_ref[:, n * d:(n + 1) * d])
        acc = acc + gate * _dot(br[...], wb_ref[n])
    mixed = _dot(acc.astype(BF16), wo_ref[...])
    o_ref[...] = _ln_rows(alpha * h + mixed, g_ref[...], beta_ref[...])


def _merge(h2d, branches, w_gate, b_gate, w_branch, w_out, ln_g, ln_b, alpha, tm):
    n, d = h2d.shape
    c = branches[0].shape[1]
    rows = lambda w: pl.BlockSpec((tm, w), lambda i: (i, 0))
    const = lambda shp: pl.BlockSpec(shp, lambda i: (0,) * len(shp))
    return pl.pallas_call(
        functools.partial(_merge_kernel, alpha=alpha),
        grid=(n // tm,),
        in_specs=[rows(d), rows(c), rows(c), rows(c), rows(c), const((d, N_BRANCH * d)),
                  const((1, N_BRANCH * d)), const((N_BRANCH, c, d)), const((d, d)),
                  const((1, d)), const((1, d))],
        out_specs=rows(d),
        out_shape=jax.ShapeDtypeStruct((n, d), F32),
        compiler_params=_cparams("parallel"),
        name="merge",
    )(h2d, *branches, w_gate, b_gate.reshape(1, -1), w_branch, w_out, ln_g.reshape(1, d),
      ln_b.reshape(1, d))


ROUTER_E0 = N_GROUPS
MOE_BLK = 256


def _routing_t(logits):
    row = lax.broadcasted_iota(I32, logits.shape, 0)
    far = jnp.int32(4 * LANES)
    neg = -jnp.inf

    def first_max(x):
        mx = jnp.max(x, axis=0, keepdims=True)
        idx = jnp.min(jnp.where(x == mx, row, far), axis=0, keepdims=True)
        return mx, idx

    lg = jnp.where(row < N_GROUPS, logits, neg)
    g_max, g_idx = first_max(lg)
    p_group = 1.0 / jnp.sum(jnp.exp(lg - g_max), axis=0, keepdims=True)
    lo = ROUTER_E0 + EXPERTS_PER_GROUP * g_idx
    le = jnp.where((row >= lo) & (row < lo + EXPERTS_PER_GROUP), logits, neg)
    m1, e1 = first_max(le)
    m2, e2 = first_max(jnp.where(row == e1, neg, le))
    r = jnp.exp(m2 - m1)
    gate1 = p_group * (1.0 / (1.0 + r))
    gate2 = p_group * (r / (1.0 + r))
    return jnp.where(row == e1, gate1, jnp.where(row == e2, gate2, 0.0)), g_idx


def _bf16_pieces(x, n):
    out, rest = [], x
    for _ in range(n):
        part = rest.astype(BF16)
        out.append(part)
        rest = rest - part.astype(F32)
    return out


def _moe_kernel(h_ref, wr_ref, br_ref, wg_ref, wu_ref, wd_ref, g_ref, beta_ref, o_ref, blk_ref, p_ref,
                xs_ref, gs_ref, acc_ref, *, alpha):
    e = pl.program_id(1)
    t, d = h_ref.shape
    f = wd_ref.shape[2]
    n_slots = p_ref.shape[0] // MOE_BLK

    @pl.when(e == 0)
    def _():
        h = h_ref[...]
        hb = h.astype(BF16)
        gates, g_idx = _routing_t(_dot_nt(wr_ref[...], hb) + br_ref[...])
        grp = lax.broadcasted_iota(I32, (SUBLANES, t), 0)
        onehot = jnp.where(grp == g_idx, 1.0, 0.0)
        r_i = lax.broadcasted_iota(I32, (t, t), 0)
        c_i = lax.broadcasted_iota(I32, (t, t), 1)
        before = jnp.where(r_i < c_i, 1.0, 0.0).astype(BF16)
        rank = _dot(onehot.astype(BF16), before)
        total = jnp.sum(onehot, axis=1, keepdims=True)
        start = jnp.int32(0)
        starts = jnp.zeros((SUBLANES, 1), F32)
        grp_col = lax.broadcasted_iota(I32, (SUBLANES, 1), 0)
        for g in range(N_GROUPS):
            n_g = (total[g, 0].astype(I32) + (MOE_BLK - 1)) // MOE_BLK
            blk_ref[2 * g] = start
            blk_ref[2 * g + 1] = n_g
            starts = jnp.where(grp_col == g, (start * MOE_BLK).astype(F32), starts)
            start = start + n_g
        blk_ref[2 * N_GROUPS] = start
        pos = jnp.sum(onehot * (starts + rank), axis=0, keepdims=True).astype(I32)
        slot = lax.broadcasted_iota(I32, (p_ref.shape[0], t), 0)
        p_ref[...] = jnp.where(slot == pos, 1.0, 0.0).astype(BF16)
        perm = p_ref[...]
        xs = _dot_nt(h.T.astype(BF16), perm).astype(BF16)
        gs = sum(_dot_nt(piece, perm) for piece in _bf16_pieces(gates, 3))
        for b in range(n_slots):
            cols = slice(b * MOE_BLK, (b + 1) * MOE_BLK)
            xs_ref[b] = xs[:, cols]
            acc_ref[b] = jnp.zeros((d, MOE_BLK), F32)
            for k in range(N_EXPERTS):
                gs_ref[b, k] = gs[ROUTER_E0 + k:ROUTER_E0 + k + 1, cols]

    grp_e = e // EXPERTS_PER_GROUP
    blk0 = blk_ref[2 * grp_e]

    def expert_block(i, _):
        b = blk0 + i
        xs = xs_ref[b]
        hg = _dot(wg_ref[0], xs)
        hu = _dot(wu_ref[0], xs)
        hid = (hg * _sigmoid(hg) * hu).astype(BF16)
        y = jnp.concatenate([_dot(wd_ref[0, :d // 2], hid), _dot(wd_ref[0, d // 2:], hid)], axis=0)
        acc_ref[b] += gs_ref[b, e] * y
        return 0

    lax.fori_loop(0, blk_ref[2 * grp_e + 1], expert_block, 0)

    @pl.when(e == pl.num_programs(1) - 1)
    def _():
        pieces = [jnp.concatenate(col, axis=1) for col in
                  zip(*[_bf16_pieces(acc_ref[b], 2) for b in range(n_slots)])]
        half = t // 2
        for c in range(2):
            cols = slice(c * half, (c + 1) * half)
            out_t = sum(_dot(piece, p_ref[:, cols]) for piece in pieces)
            o_ref[cols, :] = _ln_rows(alpha * h_ref[cols, :] + out_t.T, g_ref[...], beta_ref[...])


def _moe(h2d, w_router_t, b_router_t, wg_t, wu_t, wd_t, ln_g, ln_b, alpha, tm):
    n, d = h2d.shape
    ne, _, f = wd_t.shape
    assert tm % MOE_BLK == 0
    n_slots = tm // MOE_BLK + N_GROUPS - 1
    rows = pl.BlockSpec((tm, d), lambda i, e: (i, 0))
    const = lambda shp: pl.BlockSpec(shp, lambda i, e: (0,) * len(shp))
    return pl.pallas_call(
        functools.partial(_moe_kernel, alpha=alpha),
        grid=(n // tm, ne),
        in_specs=[rows, const((LANES, d)), const((LANES, 1)),
                  pl.BlockSpec((1, f, d), lambda i, e: (e, 0, 0)),
                  pl.BlockSpec((1, f, d), lambda i, e: (e, 0, 0)),
                  pl.BlockSpec((1, d, f), lambda i, e: (e, 0, 0)),
                  const((1, d)), const((1, d))],
        out_specs=rows,
        out_shape=jax.ShapeDtypeStruct((n, d), F32),
        scratch_shapes=[pltpu.SMEM((2 * N_GROUPS + 1,), I32),
                        pltpu.VMEM((n_slots * MOE_BLK, tm), BF16),
                        pltpu.VMEM((n_slots, d, MOE_BLK), BF16),
                        pltpu.VMEM((n_slots, ne, 1, MOE_BLK), F32),
                        pltpu.VMEM((n_slots, d, MOE_BLK), F32)],
        compiler_params=pltpu.CompilerParams(dimension_semantics=("parallel", "arbitrary"),
                                             vmem_limit_bytes=MOE_VMEM_LIMIT),
        name="moe",
    )(h2d, w_router_t, b_router_t, wg_t, wu_t, wd_t, ln_g.reshape(1, d), ln_b.reshape(1, d))


def _tiles(b, s):
    n = b * s
    pick = lambda total, want: want if total % want == 0 else total
    return dict(
        ln_tm=pick(n, 512), proj_tm=pick(s, 512), local_ts=pick(s, 512),
        fox_tq=pick(s, 256), fox_ck=pick(s, 512), dsa_tq=pick(s, 256), dsa_ck=pick(s, 512),
        merge_tm=pick(n, 1024), moe_tm=pick(n, 1024))


def kernel(x, ln_in_g, ln_in_b, w_in, b_forget, b_gate, pool_w, pool_scale, conv_dw, conv_b,
           conv_ln_g, conv_ln_b, w_branch, w_out, ln1_g, ln1_b, router_g, router_g_b, router_e,
           router_e_b, expert_w_gate, expert_w_up, expert_w_down, ln2_g, ln2_b):
    b, s, d = x.shape
    depth = w_in.shape[0]
    alpha = (2.0 * depth) ** 0.25
    t = _tiles(b, s)
    n = b * s

    h = _layer_norm(x.reshape(n, d), ln_in_g, ln_in_b, t["ln_tm"])
    for l in range(depth):
        wp = _pack_proj_weight(w_in[l])
        (pool_v, fq, fk, fv, dq, iq, glu, dk, dv, ik, small) = _projection(
            h.reshape(b, s, d), wp, t["proj_tm"])

        small_t = jnp.swapaxes(small[..., SMALL_FF:SMALL_FF + 2 * N_HEADS], 1, 2)
        f_t = jnp.pad(small_t[:, :N_HEADS], ((0, 0), (0, 8 - N_HEADS), (0, 0)))
        iw_t = jnp.pad(small_t[:, N_HEADS:], ((0, 0), (0, 8 - N_HEADS), (0, 0)))
        bf_col = jnp.pad(b_forget[l], (0, 8 - N_HEADS)).reshape(8, 1)
        c_parts = _forget_cumsum(f_t, bf_col)[:, :, :N_HEADS]

        pool_wbd = jnp.zeros((BRANCH_W, BRANCH_W), F32)
        for g in range(len(POOL_WINDOWS)):
            sl = slice(g * POOL_GROUP, (g + 1) * POOL_GROUP)
            pool_wbd = pool_wbd.at[sl, sl].set(pool_w[l, g])
        br_a, br_d = _local_mixers(pool_v, glu, pool_wbd.astype(BF16), pool_scale[l], conv_dw[l],
                                   conv_b[l], conv_ln_g[l], conv_ln_b[l], t["local_ts"])
        br_b = _fox_attention(fq, fk, fv, c_parts, t["fox_tq"], t["fox_ck"])
        br_c = _dsa_attention(dq, iq, iw_t, ik, dk, dv, t["dsa_tq"], t["dsa_ck"])

        branches = [br.reshape(n, BRANCH_W) for br in (br_a, br_b, br_c, br_d)]
        h = _merge(h, branches, w_in[l][:, GATE_OFF:].astype(BF16), b_gate[l],
                   w_branch[l].astype(BF16), w_out[l].astype(BF16), ln1_g[l], ln1_b[l], alpha,
                   t["merge_tm"])

        w_router_t = jnp.concatenate(
            [router_g[l], router_e[l], jnp.zeros((d, LANES - N_GROUPS - N_EXPERTS), F32)],
            axis=1).T.astype(BF16)
        b_router_t = jnp.concatenate(
            [router_g_b[l], router_e_b[l], jnp.zeros((LANES - N_GROUPS - N_EXPERTS,), F32)]
        ).reshape(LANES, 1)
        wg_t = jnp.swapaxes(expert_w_gate[l].astype(BF16), 1, 2)
        wu_t = jnp.swapaxes(expert_w_up[l].astype(BF16), 1, 2)
        wd_t = jnp.swapaxes(expert_w_down[l].astype(BF16), 1, 2)
        h = _moe(h, w_router_t, b_router_t, wg_t, wu_t, wd_t, ln2_g[l], ln2_b[l], alpha,
                 t["moe_tm"])
    return h.reshape(b, s, d)
```
